```python
import math
import jax, jax.numpy as jnp
from jax import lax
import numpy as np

D_MODEL = 1024
BATCH = 16
SEQ = 2048
DEPTH = 2

CTX_LEN = 256
GRID_W = 64
HEAD_DIM = 64
ROPE_THETA = 10000.0
NORM_EPS = 1e-6
Q_BLOCK = 128

DA_HEADS = 4
DA_QK = DA_HEADS * 2 * HEAD_DIM
DA_V = DA_HEADS * 2 * HEAD_DIM
DA_SCALE = HEAD_DIM ** -0.5
GQA_Q_HEADS = 8
GQA_KV_HEADS = 2
GQA_GROUP = GQA_Q_HEADS // GQA_KV_HEADS
GQA_Q = GQA_Q_HEADS * HEAD_DIM
GQA_KV = GQA_KV_HEADS * HEAD_DIM
GQA_SCALE = HEAD_DIM ** -0.5
CONV_DIM = 512
CONV_WIDTH = 3
N_BRANCH = 3
N_MOD = 6
D_FF = -(-(8 * D_MODEL) // (3 * 256)) * 256
IN_SIZES = (DA_QK, DA_QK, DA_V, GQA_Q, GQA_KV, GQA_KV, CONV_DIM, CONV_DIM, CONV_DIM, N_BRANCH * D_MODEL)
D_IN = sum(IN_SIZES)

kernel_name = "hybrid_diffattn_gqa_shortconv_adaln_prefix_block"


def rmsnorm(x, g):
    xf = x.astype(jnp.float32)
    y = xf * lax.rsqrt(jnp.mean(xf * xf, axis=-1, keepdims=True) + NORM_EPS)
    return (y * g.astype(jnp.float32)).astype(x.dtype)


def modulate(h, shift, scale):
    return h * (1.0 + scale) + shift


def split_cols(z, sizes):
    out = []
    start = 0
    for n in sizes:
        out.append(z[..., start:start + n])
        start += n
    return out


def axial_rope_tables(seq, dtype):
    rows = seq // GRID_W
    row = jnp.repeat(jnp.arange(rows, dtype=jnp.float32), GRID_W)
    col = jnp.tile(jnp.arange(GRID_W, dtype=jnp.float32), rows)
    n_freq = HEAD_DIM // 4
    inv_freq = ROPE_THETA ** (-jnp.arange(n_freq, dtype=jnp.float32) / n_freq)
    ang = jnp.stack([row[:, None] * inv_freq, col[:, None] * inv_freq], axis=1)
    return jnp.cos(ang).astype(dtype), jnp.sin(ang).astype(dtype)


def apply_axial_rope(x, rope):
    cos, sin = rope
    b, s, h, d = x.shape
    xa = x.reshape(b, s, h, 2, 2, d // 4)
    x1, x2 = xa[:, :, :, :, 0], xa[:, :, :, :, 1]
    cs, sn = cos[None, :, None], sin[None, :, None]
    out = jnp.stack([x1 * cs - x2 * sn, x2 * cs + x1 * sn], axis=4)
    return out.reshape(b, s, h, d)


def block_attention(q, k, v, scale):
    b, s, hk, g, d = q.shape
    nb = s // Q_BLOCK
    qb = jnp.moveaxis(q.reshape(b, nb, Q_BLOCK, hk, g, d), 1, 0)

    def one_block(qi):
        sc = jnp.einsum('bqhgd,bkhd->bhgqk', qi, k, preferred_element_type=jnp.float32) * scale
        p = jax.nn.softmax(sc, axis=-1)
        return jnp.einsum('bhgqk,bkhe->bqhge', p.astype(v.dtype), v)

    ob = lax.map(one_block, qb)
    return jnp.moveaxis(ob, 0, 1).reshape(b, s, hk, g, v.shape[-1])


def short_conv(u, w):
    n = u.shape[1]
    up = jnp.pad(u, ((0, 0), (1, 1), (0, 0)))
    return up[:, :n] * w[0] + up[:, 1:n + 1] * w[1] + up[:, 2:] * w[2]


def mixer_inputs(h, w_in, q_norm_g, k_norm_g, rope):
    b, s, _ = h.shape
    a_q, a_k, a_v, b_q, b_k, b_v, c_b, c_c, c_u, gates = split_cols(h @ w_in, IN_SIZES)
    a_q1, a_q2 = jnp.split(a_q.reshape(b, s, DA_HEADS, 2 * HEAD_DIM), 2, axis=-1)
    a_k1, a_k2 = jnp.split(a_k.reshape(b, s, DA_HEADS, 2 * HEAD_DIM), 2, axis=-1)
    a_v = a_v.reshape(b, s, DA_HEADS, 2 * HEAD_DIM)
    b_q = rmsnorm(b_q.reshape(b, s, GQA_Q_HEADS, HEAD_DIM), q_norm_g)
    b_k = rmsnorm(b_k.reshape(b, s, GQA_KV_HEADS, HEAD_DIM), k_norm_g)
    b_v = b_v.reshape(b, s, GQA_KV_HEADS, HEAD_DIM)
    if rope is not None:
        a_q1, a_q2, a_k1, a_k2 = (apply_axial_rope(t, rope) for t in (a_q1, a_q2, a_k1, a_k2))
        b_q, b_k = apply_axial_rope(b_q, rope), apply_axial_rope(b_k, rope)
    return (a_q1, a_q2, b_q), (a_k1, a_k2, a_v, b_k, b_v), (c_b, c_c, c_u, gates)


def mixer_outputs(queries, kv, local, conv_w, lam, lam_init, subln_g, w_a, w_b, w_c, w_out):
    a_q1, a_q2, b_q = queries
    a_k1, a_k2, a_v, b_k, b_v = kv
    c_b, c_c, c_u, gates = local
    b, s = b_q.shape[:2]
    o1 = block_attention(a_q1[:, :, :, None], a_k1, a_v, DA_SCALE)[:, :, :, 0]
    o2 = block_attention(a_q2[:, :, :, None], a_k2, a_v, DA_SCALE)[:, :, :, 0]
    y_a = (rmsnorm(o1 - lam.astype(o1.dtype) * o2, subln_g) * (1.0 - lam_init)).reshape(b, s, DA_V)
    y_b = block_attention(b_q.reshape(b, s, GQA_KV_HEADS, GQA_GROUP, HEAD_DIM), b_k, b_v, GQA_SCALE)
    y_b = y_b.reshape(b, s, GQA_Q)
    y_c = c_b * short_conv(c_c * c_u, conv_w)
    g = jax.nn.sigmoid(gates.astype(jnp.float32)).astype(gates.dtype)
    g_a, g_b, g_c = jnp.split(g, N_BRANCH, axis=-1)
    merged = g_a * (y_a @ w_a) + g_b * (y_b @ w_b) + g_c * (y_c @ w_c)
    return merged @ w_out


def swiglu(h, w_gu, w_down):
    gt, up = jnp.split(h @ w_gu, 2, axis=-1)
    return (jax.nn.silu(gt) * up) @ w_down


def setup_inputs(seed: int = 0) -> dict:
    key = jax.random.key(seed)
    ks = jax.random.split(key, 32)
    nrm = lambda k, shape, s: jax.random.normal(k, shape, jnp.float32) * s
    gain = lambda k, shape: 1.0 + 0.02 * jax.random.normal(k, shape, jnp.float32)
    L, D = DEPTH, D_MODEL
    return {
        "x": nrm(ks[0], (BATCH, SEQ, D), 1.0),
        "c": nrm(ks[1], (BATCH, D), 1.0),
        "ctx": nrm(ks[2], (BATCH, CTX_LEN, D), 1.0),
        "c_ctx": nrm(ks[3], (D,), 1.0),
        "w_mod": nrm(ks[4], (L, D, N_MOD * D), 0.3 * D ** -0.5),
        "b_mod": nrm(ks[5], (L, N_MOD * D), 0.02),
        "norm1_g": gain(ks[6], (L, D)),
        "norm2_g": gain(ks[7], (L, D)),
        "w_in": nrm(ks[8], (L, D, D_IN), D ** -0.5),
        "lam_q1": nrm(ks[9], (L, HEAD_DIM), 0.1),
        "lam_k1": nrm(ks[10], (L, HEAD_DIM), 0.1),
        "lam_q2": nrm(ks[11], (L, HEAD_DIM), 0.1),
        "lam_k2": nrm(ks[12], (L, HEAD_DIM), 0.1),
        "diff_subln_g": gain(ks[13], (L, 2 * HEAD_DIM)),
        "q_norm_g": gain(ks[14], (L, HEAD_DIM)),
        "k_norm_g": gain(ks[15], (L, HEAD_DIM)),
        "conv_w": nrm(ks[16], (L, CONV_WIDTH, CONV_DIM), CONV_WIDTH ** -0.5),
        "w_branch_a": nrm(ks[17], (L, DA_V, D), DA_V ** -0.5),
        "w_branch_b": nrm(ks[18], (L, GQA_Q, D), GQA_Q ** -0.5),
        "w_branch_c": nrm(ks[19], (L, CONV_DIM, D), CONV_DIM ** -0.5),
        "w_out": nrm(ks[20], (L, D, D), D ** -0.5),
        "w_ffn_gu": nrm(ks[21], (L, D, 2 * D_FF), D ** -0.5),
        "w_ffn_down": nrm(ks[22], (L, D_FF, D), D_FF ** -0.5),
        "final_g": gain(ks[23], (D,)),
    }


def reference(x, c, ctx, c_ctx, w_mod, b_mod, norm1_g, norm2_g, w_in, lam_q1, lam_k1, lam_q2, lam_k2,
              diff_subln_g, q_norm_g, k_norm_g, conv_w, w_branch_a, w_branch_b, w_branch_c, w_out,
              w_ffn_gu, w_ffn_down, final_g):
    seq = x.shape[1]
    rope = axial_rope_tables(seq, x.dtype)
    silu_c = jax.nn.silu(c)
    silu_cc = jax.nn.silu(c_ctx)
    for l in range(DEPTH):
        lam_init = 0.8 - 0.6 * math.exp(-0.3 * l)
        lam = (jnp.exp(jnp.sum(lam_q1[l].astype(jnp.float32) * lam_k1[l].astype(jnp.float32)))
               - jnp.exp(jnp.sum(lam_q2[l].astype(jnp.float32) * lam_k2[l].astype(jnp.float32)))
               + lam_init)
        mx = jnp.split(silu_c @ w_mod[l] + b_mod[l], N_MOD, axis=-1)
        mx = [m[:, None, :] for m in mx]
        mc = jnp.split(silu_cc @ w_mod[l] + b_mod[l], N_MOD, axis=-1)
        out_args = (conv_w[l], lam, lam_init, diff_subln_g[l], w_branch_a[l], w_branch_b[l],
                    w_branch_c[l], w_out[l])
        hc = modulate(rmsnorm(ctx, norm1_g[l]), mc[0], mc[1])
        qc, kvc, locc = mixer_inputs(hc, w_in[l], q_norm_g[l], k_norm_g[l], None)
        hx = modulate(rmsnorm(x, norm1_g[l]), mx[0], mx[1])
        qx, kvx, locx = mixer_inputs(hx, w_in[l], q_norm_g[l], k_norm_g[l], rope)
        kv_all = tuple(jnp.concatenate([kc_, kx_], axis=1) for kc_, kx_ in zip(kvc, kvx))
        x = x + mx[2] * mixer_outputs(qx, kv_all, locx, *out_args)
        x = x + mx[5] * swiglu(modulate(rmsnorm(x, norm2_g[l]), mx[3], mx[4]), w_ffn_gu[l], w_ffn_down[l])
        if l < DEPTH - 1:
            ctx = ctx + mc[2] * mixer_outputs(qc, kvc, locc, *out_args)
            ctx = ctx + mc[5] * swiglu(modulate(rmsnorm(ctx, norm2_g[l]), mc[3], mc[4]),
                                       w_ffn_gu[l], w_ffn_down[l])
    return rmsnorm(x, final_g)
```

```python
import functools
import math

import jax
import jax.numpy as jnp
from jax import lax
from jax.experimental import pallas as pl
from jax.experimental.pallas import tpu as pltpu

F32 = jnp.float32
BF16 = jnp.bfloat16

D_MODEL = 1024
DEPTH = 2
GRID_W = 64
HEAD_DIM = 64
ROPE_THETA = 10000.0
NORM_EPS = 1e-6
DA_HEADS = 4
GQA_KV_HEADS = 2
N_MOD = 6
D_FF = 2816
LANES = 128
QK_SCALE = HEAD_DIM ** -0.5

C_AQ, C_AK, C_AV, C_BQ, C_BK, C_BV, C_CB, C_CC, C_CU, C_G, C_END = (
    0, 512, 1024, 1536, 2048, 2176, 2304, 2816, 3328, 3840, 6912)

MOD_ROWS = 24
VMEM_LIMIT = 56 * 1024 * 1024

TM_IN = 256
TQ = 512
TM_POST = 256


def _params(n_axes):
    return pltpu.CompilerParams(dimension_semantics=("arbitrary",) * n_axes,
                                vmem_limit_bytes=VMEM_LIMIT)


def _resident(shape):
    nd = len(shape)
    return pl.BlockSpec(shape, lambda *_: (0,) * nd, pipeline_mode=pl.Buffered(1))


def _mod_kernel(cc_ref, w_ref, b_ref, o_ref):
    a = cc_ref[...]
    a = (a * jax.nn.sigmoid(a)).astype(BF16)
    o_ref[...] = jnp.dot(a, w_ref[...].astype(BF16), preferred_element_type=F32) + b_ref[...]


def _modulation(cc, w_mod, b_mod):
    depth, d, n = w_mod.shape
    tn = 1024
    return pl.pallas_call(
        _mod_kernel,
        grid=(depth, n // tn),
        in_specs=[pl.BlockSpec((MOD_ROWS, d), lambda l, j: (0, 0)),
                  pl.BlockSpec((None, d, tn), lambda l, j: (l, 0, j)),
                  pl.BlockSpec((None, 1, tn), lambda l, j: (l, 0, j))],
        out_specs=pl.BlockSpec((None, MOD_ROWS, tn), lambda l, j: (l, 0, j)),
        out_shape=jax.ShapeDtypeStruct((depth, MOD_ROWS, n), F32),
        compiler_params=_params(2),
        name="modulation",
    )(cc, w_mod, b_mod.reshape(depth, 1, n))


def _rms(x):
    return x * lax.rsqrt(jnp.mean(x * x, axis=-1, keepdims=True) + NORM_EPS)


def _in_kernel(x_ref, mod_ref, g1_ref, w_ref, qg_ref, kg_ref, cos_ref, s1_ref, s2_ref,
               qa_ref, ka_ref, va_ref, qb_ref, kb_ref, vb_ref, cb_ref, cu_ref, gt_ref,
               h_ref, *, use_rope):
    tm = x_ref.shape[0]
    y = _rms(x_ref[...]) * g1_ref[...]
    h_ref[...] = (y * (1.0 + mod_ref[1:2, :]) + mod_ref[0:1, :]).astype(BF16)

    def proj(c0, c1):
        return jnp.dot(h_ref[...], w_ref[:, c0:c1], preferred_element_type=F32)

    lane = lax.broadcasted_iota(jnp.int32, (tm, LANES), 1)
    low = lane < HEAD_DIM

    def rope(t):
        if not use_rope:
            return t
        return (t * cos_ref[...] + pltpu.roll(t, LANES - 16, 1) * s1_ref[...]
                + pltpu.roll(t, 16, 1) * s2_ref[...])

    def head_rms(t, g):
        tt = t * t
        s_lo = jnp.sum(jnp.where(low, tt, 0.0), axis=-1, keepdims=True)
        s_hi = jnp.sum(jnp.where(low, 0.0, tt), axis=-1, keepdims=True)
        ms = jnp.where(low, s_lo, s_hi) * (1.0 / HEAD_DIM)
        return t * lax.rsqrt(ms + NORM_EPS) * g

    def dup_halves(t):
        r = pltpu.roll(t, HEAD_DIM, 1)
        return jnp.where(low, t, r), jnp.where(low, r, t)

    z = proj(C_AQ, C_AK)
    for hd in range(DA_HEADS):
        qa_ref[hd] = (rope(z[:, hd * LANES:(hd + 1) * LANES]) * QK_SCALE).astype(BF16)
    z = proj(C_AK, C_AV)
    for hd in range(DA_HEADS):
        ka_ref[hd] = rope(z[:, hd * LANES:(hd + 1) * LANES]).astype(BF16)
    z = proj(C_AV, C_BQ)
    for hd in range(DA_HEADS):
        va_ref[hd] = z[:, hd * LANES:(hd + 1) * LANES].astype(BF16)
    z = proj(C_BQ, C_BK)
    for p in range(4):
        t = head_rms(z[:, p * LANES:(p + 1) * LANES], qg_ref[...])
        qb_ref[p] = (rope(t) * QK_SCALE).astype(BF16)
    z = proj(C_BK, C_CB)
    k0, k1 = dup_halves(rope(head_rms(z[:, :LANES], kg_ref[...])))
    kb_ref[0] = k0.astype(BF16)
    kb_ref[1] = k1.astype(BF16)
    v0, v1 = dup_halves(z[:, LANES:])
    vb_ref[0] = v0.astype(BF16)
    vb_ref[1] = v1.astype(BF16)
    z = proj(C_CB, C_G)
    cb_ref[...] = z[:, :512]
    cu_ref[...] = z[:, 512:1024] * z[:, 1024:1536]
    for j in range(3):
        c0 = C_G + j * D_MODEL
        gt_ref[:, j * D_MODEL:(j + 1) * D_MODEL] = jax.nn.sigmoid(proj(c0, c0 + D_MODEL))


def _in_stage(x, mods, mod_row, g1, w_in, qg, kg, tables, *, use_rope):
    b, s, d = x.shape
    tm = min(TM_IN, s)
    cos_t, s1_t, s2_t = tables
    if mod_row is None:
        mod_map = lambda bi, i: (bi, 0, 0)
    else:
        mod_map = lambda bi, i: (mod_row, 0, 0)
    tok = lambda w: pl.BlockSpec((None, tm, w), lambda bi, i: (bi, i, 0))
    heads = lambda n: pl.BlockSpec((None, n, tm, LANES), lambda bi, i: (bi, 0, i, 0))
    tab = pl.BlockSpec((tm, LANES), lambda bi, i: (i, 0))
    hshape = lambda n: jax.ShapeDtypeStruct((b, n, s, LANES), BF16)
    return pl.pallas_call(
        functools.partial(_in_kernel, use_rope=use_rope),
        grid=(b, s // tm),
        in_specs=[tok(d),
                  pl.BlockSpec((None, N_MOD, d), mod_map),
                  _resident((1, d)), _resident(w_in.shape),
                  _resident((1, LANES)), _resident((1, LANES)),
                  tab, tab, tab],
        out_specs=[heads(4), heads(4), heads(4), heads(4), heads(2), heads(2),
                   tok(512), tok(512), tok(3 * d)],
        out_shape=[hshape(4), hshape(4), hshape(4), hshape(4), hshape(2), hshape(2),
                   jax.ShapeDtypeStruct((b, s, 512), F32),
                   jax.ShapeDtypeStruct((b, s, 512), F32),
                   jax.ShapeDtypeStruct((b, s, 3 * d), F32)],
        scratch_shapes=[pltpu.VMEM((tm, d), BF16)],
        compiler_params=_params(2),
        name="in_proj_rope" if use_rope else "in_proj_ctx",
    )(x, mods, g1, w_in, qg, kg, cos_t, s1_t, s2_t)


def _attn_kernel(*refs, mode, has_x, lam_init):
    if mode == "da":
        lam_ref, g_ref = refs[:2]
        refs = refs[2:]
    if has_x:
        q_ref, kc_ref, vc_ref, kx_ref, vx_ref, o_ref = refs
    else:
        q_ref, kc_ref, vc_ref, o_ref = refs
    tq = q_ref.shape[0]
    q = q_ref[...]
    low = lax.broadcasted_iota(jnp.int32, (tq, LANES), 1) < HEAD_DIM
    zero = jnp.zeros_like(q)
    contract_last = (((1,), (1,)), ((), ()))

    def attend(qm):
        sc = lax.dot_general(qm, kc_ref[...], contract_last, preferred_element_type=F32)
        m = jnp.max(sc, axis=-1, keepdims=True)
        if has_x:
            sx = lax.dot_general(qm, kx_ref[...], contract_last, preferred_element_type=F32)
            m = jnp.maximum(m, jnp.max(sx, axis=-1, keepdims=True))
        pc = jnp.exp(sc - m)
        den = jnp.sum(pc, axis=-1, keepdims=True)
        o = jnp.dot(pc.astype(BF16), vc_ref[...], preferred_element_type=F32)
        if has_x:
            px = jnp.exp(sx - m)
            den = den + jnp.sum(px, axis=-1, keepdims=True)
            o = o + jnp.dot(px.astype(BF16), vx_ref[...], preferred_element_type=F32)
        return o / den

    o_lo = attend(jnp.where(low, q, zero))
    o_hi = attend(jnp.where(low, zero, q))
    if mode == "da":
        lv = lam_ref[...]
        lam = (jnp.exp(jnp.sum(lv[0:1] * lv[1:2], axis=-1, keepdims=True))
               - jnp.exp(jnp.sum(lv[2:3] * lv[3:4], axis=-1, keepdims=True)) + lam_init)
        dlt = o_lo - lam * o_hi
        o_ref[...] = (_rms(dlt) * g_ref[...] * (1.0 - lam_init)).astype(BF16)
    else:
        o_ref[...] = jnp.where(low, o_lo, o_hi).astype(BF16)


def _attn_stage(q, kc, vc, kx, vx, *, mode, lam_vec=None, subln_g=None, lam_init=0.0):
    b, nh, s, _ = q.shape
    tq = min(TQ, s)
    has_x = kx is not None
    if mode == "da":
        kv_head = lambda h: h
    else:
        kv_head = lambda h: h // 2
    q_spec = pl.BlockSpec((None, None, tq, LANES), lambda bi, h, i: (bi, h, i, 0))
    kv_spec = lambda n: pl.BlockSpec((None, None, n, LANES), lambda bi, h, i: (bi, kv_head(h), 0, 0))
    args, specs = [], []
    if mode == "da":
        args += [lam_vec, subln_g]
        specs += [pl.BlockSpec(lam_vec.shape, lambda bi, h, i: (0, 0)),
                  pl.BlockSpec(subln_g.shape, lambda bi, h, i: (0, 0))]
    args += [q, kc, vc]
    specs += [q_spec, kv_spec(kc.shape[2]), kv_spec(kc.shape[2])]
    if has_x:
        args += [kx, vx]
        specs += [kv_spec(kx.shape[2]), kv_spec(kx.shape[2])]
    return pl.pallas_call(
        functools.partial(_attn_kernel, mode=mode, has_x=has_x, lam_init=lam_init),
        grid=(b, nh, s // tq),
        in_specs=specs,
        out_specs=pl.BlockSpec((None, tq, LANES), lambda bi, h, i: (bi, i, h)),
        out_shape=jax.ShapeDtypeStruct((b, s, nh * LANES), BF16),
        compiler_params=_params(3),
        name=f"attn_{mode}_{'x' if has_x else 'ctx'}",
    )(*args)


def _post_kernel(*refs, final, n_tiles):
    (x_ref, ya_ref, yb_ref, cb_ref, cu_ref, cup_ref, cun_ref, gt_ref, mod_ref, g2_ref, cw_ref,
     wa_ref, wb_ref, wc_ref, wo_ref, wgu_ref, wd_ref) = refs[:17]
    if final:
        gf_ref, o_ref = refs[17:]
    else:
        (o_ref,) = refs[17:]
    tm = x_ref.shape[0]
    i = pl.program_id(1)

    cu = cu_ref[...]
    row = lax.broadcasted_iota(jnp.int32, cu.shape, 0)
    prev_row = jnp.where(i > 0, cup_ref[7:8, :], 0.0)
    next_row = jnp.where(i < n_tiles - 1, cun_ref[0:1, :], 0.0)
    before = jnp.where(row == 0, prev_row, pltpu.roll(cu, 1, 0))
    after = jnp.where(row == tm - 1, next_row, pltpu.roll(cu, tm - 1, 0))
    yc = cb_ref[...] * (before * cw_ref[0:1, :] + cu * cw_ref[1:2, :] + after * cw_ref[2:3, :])

    d = D_MODEL
    merged = (gt_ref[:, 0:d] * jnp.dot(ya_ref[...], wa_ref[...], preferred_element_type=F32)
              + gt_ref[:, d:2 * d] * jnp.dot(yb_ref[...], wb_ref[...], preferred_element_type=F32)
              + gt_ref[:, 2 * d:3 * d] * jnp.dot(yc.astype(BF16), wc_ref[...],
                                                 preferred_element_type=F32))
    mix = jnp.dot(merged.astype(BF16), wo_ref[...], preferred_element_type=F32)
    x1 = x_ref[...] + mod_ref[2:3, :] * mix

    h2 = (_rms(x1) * g2_ref[...] * (1.0 + mod_ref[4:5, :]) + mod_ref[3:4, :]).astype(BF16)
    gu = jnp.dot(h2, wgu_ref[...], preferred_element_type=F32)
    gate = gu[:, :D_FF]
    act = (gate * jax.nn.sigmoid(gate) * gu[:, D_FF:]).astype(BF16)
    x2 = x1 + mod_ref[5:6, :] * jnp.dot(act, wd_ref[...], preferred_element_type=F32)
    if final:
        x2 = _rms(x2) * gf_ref[...]
    o_ref[...] = x2


def _post_stage(x, ya, yb, cb, cu, gates, mods, mod_row, g2, conv_w, wa, wb, wc, wo, wgu, wd,
                final_g=None):
    b, s, d = x.shape
    tm = min(TM_POST, s)
    n_tiles = s // tm
    rows8 = tm // 8
    last8 = s // 8 - 1
    if mod_row is None:
        mod_map = lambda bi, i: (bi, 0, 0)
    else:
        mod_map = lambda bi, i: (mod_row, 0, 0)
    tok = lambda w: pl.BlockSpec((None, tm, w), lambda bi, i: (bi, i, 0))
    halo_prev = pl.BlockSpec((None, 8, 512), lambda bi, i: (bi, jnp.maximum(i * rows8 - 1, 0), 0))
    halo_next = pl.BlockSpec((None, 8, 512),
                             lambda bi, i: (bi, jnp.minimum((i + 1) * rows8, last8), 0))
    args = [x, ya, yb, cb, cu, cu, cu, gates, mods, g2, conv_w, wa, wb, wc, wo, wgu, wd]
    specs = [tok(d), tok(512), tok(512), tok(512), tok(512), halo_prev, halo_next, tok(3 * d),
             pl.BlockSpec((None, N_MOD, d), mod_map), _resident((1, d)), _resident(conv_w.shape),
             _resident(wa.shape), _resident(wb.shape), _resident(wc.shape), _resident(wo.shape),
             _resident(wgu.shape), _resident(wd.shape)]
    final = final_g is not None
    if final:
        args.append(final_g)
        specs.append(_resident((1, d)))
    return pl.pallas_call(
        functools.partial(_post_kernel, final=final, n_tiles=n_tiles),
        grid=(b, n_tiles),
        in_specs=specs,
        out_specs=tok(d),
        out_shape=jax.ShapeDtypeStruct((b, s, d), F32),
        compiler_params=_params(2),
        name="post_final" if final else "post",
    )(*args)


def _rope_tables(seq):
    rows = seq // GRID_W
    row = jnp.repeat(jnp.arange(rows, dtype=F32), GRID_W)
    col = jnp.tile(jnp.arange(GRID_W, dtype=F32), rows)
    n_freq = HEAD_DIM // 4
    inv_freq = ROPE_THETA ** (-jnp.arange(n_freq, dtype=F32) / n_freq)
    ang_r, ang_c = row[:, None] * inv_freq, col[:, None] * inv_freq
    cr, sr, cc, sc = jnp.cos(ang_r), jnp.sin(ang_r), jnp.cos(ang_c), jnp.sin(ang_c)
    z = jnp.zeros_like(sr)
    rep = lambda parts: jnp.tile(jnp.concatenate(parts, axis=1), (1, LANES // HEAD_DIM))
    return rep([cr, cr, cc, cc]), rep([-sr, z, -sc, z]), rep([z, sr, z, sc])


def kernel(x, c, ctx, c_ctx, w_mod, b_mod, norm1_g, norm2_g, w_in, lam_q1, lam_k1, lam_q2, lam_k2,
           diff_subln_g, q_norm_g, k_norm_g, conv_w, w_branch_a, w_branch_b, w_branch_c, w_out,
           w_ffn_gu, w_ffn_down, final_g):
    batch, seq, d = x.shape
    ctx_row = batch
    cc = jnp.concatenate([c, c_ctx[None, :], jnp.zeros((MOD_ROWS - batch - 1, d), F32)], axis=0)
    mods_all = _modulation(cc, w_mod, b_mod).reshape(DEPTH, MOD_ROWS, N_MOD, d)
    tables = _rope_tables(seq)
    row1 = lambda v: v.reshape(1, -1)
    pair = lambda v: jnp.tile(v, LANES // HEAD_DIM).reshape(1, LANES)

    for l in range(DEPTH):
        lam_init = 0.8 - 0.6 * math.exp(-0.3 * l)
        mods = mods_all[l]
        lam_vec = jnp.stack([lam_q1[l], lam_k1[l], lam_q2[l], lam_k2[l]]).astype(F32)
        w_in_l = w_in[l].astype(BF16)
        in_args = (row1(norm1_g[l]), w_in_l, pair(q_norm_g[l]), pair(k_norm_g[l]), tables)
        post_w = (row1(norm2_g[l]), conv_w[l], w_branch_a[l].astype(BF16),
                  w_branch_b[l].astype(BF16), w_branch_c[l].astype(BF16), w_out[l].astype(BF16),
                  w_ffn_gu[l].astype(BF16), w_ffn_down[l].astype(BF16))
        da = dict(mode="da", lam_vec=lam_vec, subln_g=row1(diff_subln_g[l]), lam_init=lam_init)

        qa_c, ka_c, va_c, qb_c, kb_c, vb_c, cb_c, cu_c, gt_c = _in_stage(
            ctx, mods, ctx_row, *in_args, use_rope=False)
        qa, ka, va, qb, kb, vb, cb, cu, gt = _in_stage(x, mods, None, *in_args, use_rope=True)

        ya = _attn_stage(qa, ka_c, va_c, ka, va, **da)
        yb = _attn_stage(qb, kb_c, vb_c, kb, vb, mode="gqa")
        last = l == DEPTH - 1
        x = _post_stage(x, ya, yb, cb, cu, gt, mods, None, *post_w,
                        final_g=row1(final_g) if last else None)
        if not last:
            ya_c = _attn_stage(qa_c, ka_c, va_c, None, None, **da)
            yb_c = _attn_stage(qb_c, kb_c, vb_c, None, None, mode="gqa")
            ctx = _post_stage(ctx, ya_c, yb_c, cb_c, cu_c, gt_c, mods, ctx_row, *post_w)
    return x
```

```python
import functools
import math

import jax
import jax.numpy as jnp
from jax import lax
from jax.experimental import pallas as pl
from jax.experimental.pallas import tpu as pltpu

F32 = jnp.float32
BF16 = jnp.bfloat16

D_MODEL = 1024
DEPTH = 2
GRID_W = 64
HEAD_DIM = 64
ROPE_THETA = 10000.0
NORM_EPS = 1e-6
DA_HEADS = 4
GQA_KV_HEADS = 2
N_MOD = 6
D_FF = 2816
LANES = 128
QK_SCALE = HEAD_DIM ** -0.5

C_AQ, C_AK, C_AV, C_BQ, C_BK, C_BV, C_CB, C_CC, C_CU, C_G, C_END = (
    0, 512, 1024, 1536, 2048, 2176, 2304, 2816, 3328, 3840, 6912)

MOD_ROWS = 24
VMEM_LIMIT = 56 * 1024 * 1024

TM_IN = 256
TQ = 512
ATTN_ROWS = 64
ATTN_AHEAD = 1
TM_POST = 256


def _params(n_axes):
    return pltpu.CompilerParams(dimension_semantics=("arbitrary",) * n_axes,
                                vmem_limit_bytes=VMEM_LIMIT)


def _resident(shape):
    nd = len(shape)
    return pl.BlockSpec(shape, lambda *_: (0,) * nd, pipeline_mode=pl.Buffered(1))


def _mod_kernel(cc_ref, w_ref, b_ref, o_ref):
    a = cc_ref[...]
    a = (a * jax.nn.sigmoid(a)).astype(BF16)
    o_ref[...] = jnp.dot(a, w_ref[...].astype(BF16), preferred_element_type=F32) + b_ref[...]


def _modulation(cc, w_mod, b_mod):
    depth, d, n = w_mod.shape
    tn = 1024
    return pl.pallas_call(
        _mod_kernel,
        grid=(depth, n // tn),
        in_specs=[pl.BlockSpec((MOD_ROWS, d), lambda l, j: (0, 0)),
                  pl.BlockSpec((None, d, tn), lambda l, j: (l, 0, j)),
                  pl.BlockSpec((None, 1, tn), lambda l, j: (l, 0, j))],
        out_specs=pl.BlockSpec((None, MOD_ROWS, tn), lambda l, j: (l, 0, j)),
        out_shape=jax.ShapeDtypeStruct((depth, MOD_ROWS, n), F32),
        compiler_params=_params(2),
        name="modulation",
    )(cc, w_mod, b_mod.reshape(depth, 1, n))


def _rms(x):
    return x * lax.rsqrt(jnp.mean(x * x, axis=-1, keepdims=True) + NORM_EPS)


def _in_kernel(x_ref, mod_ref, g1_ref, w_ref, qg_ref, kg_ref, cos_ref, s1_ref, s2_ref,
               qa_ref, ka_ref, va_ref, qb_ref, kb_ref, vb_ref, cb_ref, cu_ref, gt_ref,
               h_ref, *, use_rope):
    tm = x_ref.shape[0]
    y = _rms(x_ref[...]) * g1_ref[...]
    h_ref[...] = (y * (1.0 + mod_ref[1:2, :]) + mod_ref[0:1, :]).astype(BF16)

    def proj(c0, c1):
        return jnp.dot(h_ref[...], w_ref[:, c0:c1], preferred_element_type=F32)

    lane = lax.broadcasted_iota(jnp.int32, (tm, LANES), 1)
    low = lane < HEAD_DIM

    def rope(t):
        if not use_rope:
            return t
        return (t * cos_ref[...] + pltpu.roll(t, LANES - 16, 1) * s1_ref[...]
                + pltpu.roll(t, 16, 1) * s2_ref[...])

    def head_rms(t, g):
        tt = t * t
        s_lo = jnp.sum(jnp.where(low, tt, 0.0), axis=-1, keepdims=True)
        s_hi = jnp.sum(jnp.where(low, 0.0, tt), axis=-1, keepdims=True)
        ms = jnp.where(low, s_lo, s_hi) * (1.0 / HEAD_DIM)
        return t * lax.rsqrt(ms + NORM_EPS) * g

    def dup_halves(t):
        r = pltpu.roll(t, HEAD_DIM, 1)
        return jnp.where(low, t, r), jnp.where(low, r, t)

    z = proj(C_AQ, C_AK)
    for hd in range(DA_HEADS):
        qa_ref[hd] = (rope(z[:, hd * LANES:(hd + 1) * LANES]) * QK_SCALE).astype(BF16)
    z = proj(C_AK, C_AV)
    for hd in range(DA_HEADS):
        ka_ref[hd] = rope(z[:, hd * LANES:(hd + 1) * LANES]).astype(BF16)
    z = proj(C_AV, C_BQ)
    for hd in range(DA_HEADS):
        va_ref[hd] = z[:, hd * LANES:(hd + 1) * LANES].astype(BF16)
    z = proj(C_BQ, C_BK)
    for p in range(4):
        t = head_rms(z[:, p * LANES:(p + 1) * LANES], qg_ref[...])
        qb_ref[p] = (rope(t) * QK_SCALE).astype(BF16)
    z = proj(C_BK, C_CB)
    k0, k1 = dup_halves(rope(head_rms(z[:, :LANES], kg_ref[...])))
    kb_ref[0] = k0.astype(BF16)
    kb_ref[1] = k1.astype(BF16)
    v0, v1 = dup_halves(z[:, LANES:])
    vb_ref[0] = v0.astype(BF16)
    vb_ref[1] = v1.astype(BF16)
    z = proj(C_CB, C_G)
    cb_ref[...] = z[:, :512]
    cu_ref[...] = z[:, 512:1024] * z[:, 1024:1536]
    for j in range(3):
        c0 = C_G + j * D_MODEL
        gt_ref[:, j * D_MODEL:(j + 1) * D_MODEL] = jax.nn.sigmoid(proj(c0, c0 + D_MODEL))


def _in_stage(x, mods, mod_row, g1, w_in, qg, kg, tables, *, use_rope):
    b, s, d = x.shape
    tm = min(TM_IN, s)
    cos_t, s1_t, s2_t = tables
    if mod_row is None:
        mod_map = lambda bi, i: (bi, 0, 0)
    else:
        mod_map = lambda bi, i: (mod_row, 0, 0)
    tok = lambda w: pl.BlockSpec((None, tm, w), lambda bi, i: (bi, i, 0))
    heads = lambda n: pl.BlockSpec((None, n, tm, LANES), lambda bi, i: (bi, 0, i, 0))
    tab = pl.BlockSpec((tm, LANES), lambda bi, i: (i, 0))
    hshape = lambda n: jax.ShapeDtypeStruct((b, n, s, LANES), BF16)
    return pl.pallas_call(
        functools.partial(_in_kernel, use_rope=use_rope),
        grid=(b, s // tm),
        in_specs=[tok(d),
                  pl.BlockSpec((None, N_MOD, d), mod_map),
                  _resident((1, d)), _resident(w_in.shape),
                  _resident((1, LANES)), _resident((1, LANES)),
                  tab, tab, tab],
        out_specs=[heads(4), heads(4), heads(4), heads(4), heads(2), heads(2),
                   tok(512), tok(512), tok(3 * d)],
        out_shape=[hshape(4), hshape(4), hshape(4), hshape(4), hshape(2), hshape(2),
                   jax.ShapeDtypeStruct((b, s, 512), F32),
                   jax.ShapeDtypeStruct((b, s, 512), F32),
                   jax.ShapeDtypeStruct((b, s, 3 * d), F32)],
        scratch_shapes=[pltpu.VMEM((tm, d), BF16)],
        compiler_params=_params(2),
        name="in_proj_rope" if use_rope else "in_proj_ctx",
    )(x, mods, g1, w_in, qg, kg, cos_t, s1_t, s2_t)


def _attn_kernel(*refs, mode, has_x, lam_init):
    if mode == "da":
        lam_ref, g_ref = refs[:2]
        refs = refs[2:]
    if has_x:
        q_ref, kc_ref, vc_ref, kx_ref, vx_ref, o_ref = refs
    else:
        q_ref, kc_ref, vc_ref, o_ref = refs
    tq = q_ref.shape[0]
    rb = min(ATTN_ROWS, tq)
    low = lax.broadcasted_iota(jnp.int32, (rb, LANES), 1) < HEAD_DIM
    contract_last = (((1,), (1,)), ((), ()))
    if mode == "da":
        lv = lam_ref[...]
        lam = (jnp.exp(jnp.sum(lv[0:1] * lv[1:2], axis=-1, keepdims=True))
               - jnp.exp(jnp.sum(lv[2:3] * lv[3:4], axis=-1, keepdims=True)) + lam_init)

    def scores(r):
        q = q_ref[r * rb:(r + 1) * rb, :]
        zero = jnp.zeros_like(q)
        qm = jnp.concatenate([jnp.where(low, q, zero), jnp.where(low, zero, q)], axis=0)
        sc = lax.dot_general(qm, kc_ref[...], contract_last, preferred_element_type=F32)
        sx = None
        if has_x:
            sx = lax.dot_general(qm, kx_ref[...], contract_last, preferred_element_type=F32)
        return sc, sx

    def finish(r, sc, sx):
        m = jnp.max(sc, axis=-1, keepdims=True)
        if has_x:
            m = jnp.maximum(m, jnp.max(sx, axis=-1, keepdims=True))
        pc = jnp.exp(sc - m)
        den = jnp.sum(pc, axis=-1, keepdims=True)
        o = jnp.dot(pc.astype(BF16), vc_ref[...], preferred_element_type=F32)
        if has_x:
            px = jnp.exp(sx - m)
            den = den + jnp.sum(px, axis=-1, keepdims=True)
            o = o + jnp.dot(px.astype(BF16), vx_ref[...], preferred_element_type=F32)
        o = o / den
        o_lo, o_hi = o[:rb], o[rb:]
        if mode == "da":
            dlt = o_lo - lam * o_hi
            res = _rms(dlt) * g_ref[...] * (1.0 - lam_init)
        else:
            res = jnp.where(low, o_lo, o_hi)
        o_ref[r * rb:(r + 1) * rb, :] = res.astype(BF16)

    n_blocks = tq // rb
    pending = [scores(r) for r in range(min(ATTN_AHEAD, n_blocks))]
    for r in range(n_blocks):
        if r + ATTN_AHEAD < n_blocks:
            pending.append(scores(r + ATTN_AHEAD))
        finish(r, *pending.pop(0))


def _attn_stage(q, kc, vc, kx, vx, *, mode, lam_vec=None, subln_g=None, lam_init=0.0):
    b, nh, s, _ = q.shape
    tq = min(TQ, s)
    has_x = kx is not None
    if mode == "da":
        kv_head = lambda h: h
    else:
        kv_head = lambda h: h // 2
    q_spec = pl.BlockSpec((None, None, tq, LANES), lambda bi, h, i: (bi, h, i, 0))
    kv_spec = lambda n: pl.BlockSpec((None, None, n, LANES), lambda bi, h, i: (bi, kv_head(h), 0, 0))
    args, specs = [], []
    if mode == "da":
        args += [lam_vec, subln_g]
        specs += [pl.BlockSpec(lam_vec.shape, lambda bi, h, i: (0, 0)),
                  pl.BlockSpec(subln_g.shape, lambda bi, h, i: (0, 0))]
    args += [q, kc, vc]
    specs += [q_spec, kv_spec(kc.shape[2]), kv_spec(kc.shape[2])]
    if has_x:
        args += [kx, vx]
        specs += [kv_spec(kx.shape[2]), kv_spec(kx.shape[2])]
    return pl.pallas_call(
        functools.partial(_attn_kernel, mode=mode, has_x=has_x, lam_init=lam_init),
        grid=(b, nh, s // tq),
        in_specs=specs,
        out_specs=pl.BlockSpec((None, tq, LANES), lambda bi, h, i: (bi, i, h)),
        out_shape=jax.ShapeDtypeStruct((b, s, nh * LANES), BF16),
        compiler_params=_params(3),
        name=f"attn_{mode}_{'x' if has_x else 'ctx'}",
    )(*args)


def _post_kernel(*refs, final, n_tiles):
    (x_ref, ya_ref, yb_ref, cb_ref, cu_ref, cup_ref, cun_ref, gt_ref, mod_ref, g2_ref, cw_ref,
     wa_ref, wb_ref, wc_ref, wo_ref, wgu_ref, wd_ref) = refs[:17]
    if final:
        gf_ref, o_ref = refs[17:]
    else:
        (o_ref,) = refs[17:]
    tm = x_ref.shape[0]
    i = pl.program_id(1)

    cu = cu_ref[...]
    row = lax.broadcasted_iota(jnp.int32, cu.shape, 0)
    prev_row = jnp.where(i > 0, cup_ref[7:8, :], 0.0)
    next_row = jnp.where(i < n_tiles - 1, cun_ref[0:1, :], 0.0)
    before = jnp.where(row == 0, prev_row, pltpu.roll(cu, 1, 0))
    after = jnp.where(row == tm - 1, next_row, pltpu.roll(cu, tm - 1, 0))
    yc = cb_ref[...] * (before * cw_ref[0:1, :] + cu * cw_ref[1:2, :] + after * cw_ref[2:3, :])

    d = D_MODEL
    merged = (gt_ref[:, 0:d] * jnp.dot(ya_ref[...], wa_ref[...], preferred_element_type=F32)
              + gt_ref[:, d:2 * d] * jnp.dot(yb_ref[...], wb_ref[...], preferred_element_type=F32)
              + gt_ref[:, 2 * d:3 * d] * jnp.dot(yc.astype(BF16), wc_ref[...],
                                                 preferred_element_type=F32))
    mix = jnp.dot(merged.astype(BF16), wo_ref[...], preferred_element_type=F32)
    x1 = x_ref[...] + mod_ref[2:3, :] * mix

    h2 = (_rms(x1) * g2_ref[...] * (1.0 + mod_ref[4:5, :]) + mod_ref[3:4, :]).astype(BF16)
    gu = jnp.dot(h2, wgu_ref[...], preferred_element_type=F32)
    gate = gu[:, :D_FF]
    act = (gate * jax.nn.sigmoid(gate) * gu[:, D_FF:]).astype(BF16)
    x2 = x1 + mod_ref[5:6, :] * jnp.dot(act, wd_ref[...], preferred_element_type=F32)
    if final:
        x2 = _rms(x2) * gf_ref[...]
    o_ref[...] = x2


def _post_stage(x, ya, yb, cb, cu, gates, mods, mod_row, g2, conv_w, wa, wb, wc, wo, wgu, wd,
                final_g=None):
    b, s, d = x.shape
    tm = min(TM_POST, s)
    n_tiles = s // tm
    rows8 = tm // 8
    last8 = s // 8 - 1
    if mod_row is None:
        mod_map = lambda bi, i: (bi, 0, 0)
    else:
        mod_map = lambda bi, i: (mod_row, 0, 0)
    tok = lambda w: pl.BlockSpec((None, tm, w), lambda bi, i: (bi, i, 0))
    halo_prev = pl.BlockSpec((None, 8, 512), lambda bi, i: (bi, jnp.maximum(i * rows8 - 1, 0), 0))
    halo_next = pl.BlockSpec((None, 8, 512),
                             lambda bi, i: (bi, jnp.minimum((i + 1) * rows8, last8), 0))
    args = [x, ya, yb, cb, cu, cu, cu, gates, mods, g2, conv_w, wa, wb, wc, wo, wgu, wd]
    specs = [tok(d), tok(512), tok(512), tok(512), tok(512), halo_prev, halo_next, tok(3 * d),
             pl.BlockSpec((None, N_MOD, d), mod_map), _resident((1, d)), _resident(conv_w.shape),
             _resident(wa.shape), _resident(wb.shape), _resident(wc.shape), _resident(wo.shape),
             _resident(wgu.shape), _resident(wd.shape)]
    final = final_g is not None
    if final:
        args.append(final_g)
        specs.append(_resident((1, d)))
    return pl.pallas_call(
        functools.partial(_post_kernel, final=final, n_tiles=n_tiles),
        grid=(b, n_tiles),
        in_specs=specs,
        out_specs=tok(d),
        out_shape=jax.ShapeDtypeStruct((b, s, d), F32),
        compiler_params=_params(2),
        name="post_final" if final else "post",
    )(*args)


def _rope_tables(seq):
    rows = seq // GRID_W
    row = jnp.repeat(jnp.arange(rows, dtype=F32), GRID_W)
    col = jnp.tile(jnp.arange(GRID_W, dtype=F32), rows)
    n_freq = HEAD_DIM // 4
    inv_freq = ROPE_THETA ** (-jnp.arange(n_freq, dtype=F32) / n_freq)
    ang_r, ang_c = row[:, None] * inv_freq, col[:, None] * inv_freq
    cr, sr, cc, sc = jnp.cos(ang_r), jnp.sin(ang_r), jnp.cos(ang_c), jnp.sin(ang_c)
    z = jnp.zeros_like(sr)
    rep = lambda parts: jnp.tile(jnp.concatenate(parts, axis=1), (1, LANES // HEAD_DIM))
    return rep([cr, cr, cc, cc]), rep([-sr, z, -sc, z]), rep([z, sr, z, sc])


def kernel(x, c, ctx, c_ctx, w_mod, b_mod, norm1_g, norm2_g, w_in, lam_q1, lam_k1, lam_q2, lam_k2,
           diff_subln_g, q_norm_g, k_norm_g, conv_w, w_branch_a, w_branch_b, w_branch_c, w_out,
           w_ffn_gu, w_ffn_down, final_g):
    batch, seq, d = x.shape
    ctx_row = batch
    cc = jnp.concatenate([c, c_ctx[None, :], jnp.zeros((MOD_ROWS - batch - 1, d), F32)], axis=0)
    mods_all = _modulation(cc, w_mod, b_mod).reshape(DEPTH, MOD_ROWS, N_MOD, d)
    tables = _rope_tables(seq)
    row1 = lambda v: v.reshape(1, -1)
    pair = lambda v: jnp.tile(v, LANES // HEAD_DIM).reshape(1, LANES)

    for l in range(DEPTH):
        lam_init = 0.8 - 0.6 * math.exp(-0.3 * l)
        mods = mods_all[l]
        lam_vec = jnp.stack([lam_q1[l], lam_k1[l], lam_q2[l], lam_k2[l]]).astype(F32)
        w_in_l = w_in[l].astype(BF16)
        in_args = (row1(norm1_g[l]), w_in_l, pair(q_norm_g[l]), pair(k_norm_g[l]), tables)
        post_w = (row1(norm2_g[l]), conv_w[l], w_branch_a[l].astype(BF16),
                  w_branch_b[l].astype(BF16), w_branch_c[l].astype(BF16), w_out[l].astype(BF16),
                  w_ffn_gu[l].astype(BF16), w_ffn_down[l].astype(BF16))
        da = dict(mode="da", lam_vec=lam_vec, subln_g=row1(diff_subln_g[l]), lam_init=lam_init)

        qa_c, ka_c, va_c, qb_c, kb_c, vb_c, cb_c, cu_c, gt_c = _in_stage(
            ctx, mods, ctx_row, *in_args, use_rope=False)
        qa, ka, va, qb, kb, vb, cb, cu, gt = _in_stage(x, mods, None, *in_args, use_rope=True)

        ya = _attn_stage(qa, ka_c, va_c, ka, va, **da)
        yb = _attn_stage(qb, kb_c, vb_c, kb, vb, mode="gqa")
        last = l == DEPTH - 1
        x = _post_stage(x, ya, yb, cb, cu, gt, mods, None, *post_w,
                        final_g=row1(final_g) if last else None)
        if not last:
            ya_c = _attn_stage(qa_c, ka_c, va_c, None, None, **da)
            yb_c = _attn_stage(qb_c, kb_c, vb_c, None, None, mode="gqa")
            ctx = _post_stage(ctx, ya_c, yb_c, cb_c, cu_c, gt_c, mods, ctx_row, *post_w)
    return x
```

```python
import functools
import math

import jax
import jax.numpy as jnp
from jax import lax
from jax.experimental import pallas as pl
from jax.experimental.pallas import tpu as pltpu

F32 = jnp.float32
BF16 = jnp.bfloat16

D_MODEL = 1024
DEPTH = 2
GRID_W = 64
HEAD_DIM = 64
ROPE_THETA = 10000.0
NORM_EPS = 1e-6
DA_HEADS = 4
GQA_KV_HEADS = 2
N_MOD = 6
D_FF = 2816
LANES = 128
QK_SCALE = HEAD_DIM ** -0.5

C_AQ, C_AK, C_AV, C_BQ, C_BK, C_BV, C_CB, C_CC, C_CU, C_G, C_END = (
    0, 512, 1024, 1536, 2048, 2176, 2304, 2816, 3328, 3840, 6912)

MOD_ROWS = 24
VMEM_LIMIT = 56 * 1024 * 1024

TM_IN = 256
TQ = 2048
ATTN_ROWS = 64
ATTN_AHEAD = 1
TM_POST = 256


def _params(n_axes):
    return pltpu.CompilerParams(dimension_semantics=("arbitrary",) * n_axes,
                                vmem_limit_bytes=VMEM_LIMIT)


def _resident(shape):
    nd = len(shape)
    return pl.BlockSpec(shape, lambda *_: (0,) * nd, pipeline_mode=pl.Buffered(1))


def _mod_kernel(cc_ref, w_ref, b_ref, o_ref):
    a = cc_ref[...]
    a = (a * jax.nn.sigmoid(a)).astype(BF16)
    o_ref[...] = jnp.dot(a, w_ref[...].astype(BF16), preferred_element_type=F32) + b_ref[...]


def _modulation(cc, w_mod, b_mod):
    depth, d, n = w_mod.shape
    tn = 1024
    return pl.pallas_call(
        _mod_kernel,
        grid=(depth, n // tn),
        in_specs=[pl.BlockSpec((MOD_ROWS, d), lambda l, j: (0, 0)),
                  pl.BlockSpec((None, d, tn), lambda l, j: (l, 0, j)),
                  pl.BlockSpec((None, 1, tn), lambda l, j: (l, 0, j))],
        out_specs=pl.BlockSpec((None, MOD_ROWS, tn), lambda l, j: (l, 0, j)),
        out_shape=jax.ShapeDtypeStruct((depth, MOD_ROWS, n), F32),
        compiler_params=_params(2),
        name="modulation",
    )(cc, w_mod, b_mod.reshape(depth, 1, n))


def _rms(x):
    return x * lax.rsqrt(jnp.mean(x * x, axis=-1, keepdims=True) + NORM_EPS)


def _in_kernel(x_ref, mod_ref, g1_ref, w_ref, qg_ref, kg_ref, cos_ref, s1_ref, s2_ref,
               *refs, use_rope, kv_only):
    if kv_only:
        ka_ref, va_ref, kb_ref, vb_ref, h_ref = refs
    else:
        qa_ref, ka_ref, va_ref, qb_ref, kb_ref, vb_ref, cb_ref, cu_ref, gt_ref, h_ref = refs
    tm = x_ref.shape[0]
    y = _rms(x_ref[...]) * g1_ref[...]
    h_ref[...] = (y * (1.0 + mod_ref[1:2, :]) + mod_ref[0:1, :]).astype(BF16)

    def proj(c0, c1):
        return jnp.dot(h_ref[...], w_ref[:, c0:c1], preferred_element_type=F32)

    lane = lax.broadcasted_iota(jnp.int32, (tm, LANES), 1)
    low = lane < HEAD_DIM

    def rope(t):
        if not use_rope:
            return t
        return (t * cos_ref[...] + pltpu.roll(t, LANES - 16, 1) * s1_ref[...]
                + pltpu.roll(t, 16, 1) * s2_ref[...])

    def head_rms(t, g):
        tt = t * t
        s_lo = jnp.sum(jnp.where(low, tt, 0.0), axis=-1, keepdims=True)
        s_hi = jnp.sum(jnp.where(low, 0.0, tt), axis=-1, keepdims=True)
        ms = jnp.where(low, s_lo, s_hi) * (1.0 / HEAD_DIM)
        return t * lax.rsqrt(ms + NORM_EPS) * g

    def dup_halves(t):
        r = pltpu.roll(t, HEAD_DIM, 1)
        return jnp.where(low, t, r), jnp.where(low, r, t)

    if not kv_only:
        z = proj(C_AQ, C_AK)
        for hd in range(DA_HEADS):
            qa_ref[hd] = (rope(z[:, hd * LANES:(hd + 1) * LANES]) * QK_SCALE).astype(BF16)
    z = proj(C_AK, C_AV)
    for hd in range(DA_HEADS):
        ka_ref[hd] = rope(z[:, hd * LANES:(hd + 1) * LANES]).astype(BF16)
    z = proj(C_AV, C_BQ)
    for hd in range(DA_HEADS):
        va_ref[hd] = z[:, hd * LANES:(hd + 1) * LANES].astype(BF16)
    if not kv_only:
        z = proj(C_BQ, C_BK)
        for p in range(4):
            t = head_rms(z[:, p * LANES:(p + 1) * LANES], qg_ref[...])
            qb_ref[p] = (rope(t) * QK_SCALE).astype(BF16)
    z = proj(C_BK, C_CB)
    k0, k1 = dup_halves(rope(head_rms(z[:, :LANES], kg_ref[...])))
    kb_ref[0] = k0.astype(BF16)
    kb_ref[1] = k1.astype(BF16)
    v0, v1 = dup_halves(z[:, LANES:])
    vb_ref[0] = v0.astype(BF16)
    vb_ref[1] = v1.astype(BF16)
    if kv_only:
        return
    z = proj(C_CB, C_G)
    cb_ref[...] = z[:, :512]
    cu_ref[...] = z[:, 512:1024] * z[:, 1024:1536]
    for j in range(3):
        c0 = C_G + j * D_MODEL
        gt_ref[:, j * D_MODEL:(j + 1) * D_MODEL] = jax.nn.sigmoid(proj(c0, c0 + D_MODEL))


def _in_stage(x, mods, mod_row, g1, w_in, qg, kg, tables, *, use_rope, kv_only=False):
    b, s, d = x.shape
    tm = min(TM_IN, s)
    cos_t, s1_t, s2_t = tables
    if mod_row is None:
        mod_map = lambda bi, i: (bi, 0, 0)
    else:
        mod_map = lambda bi, i: (mod_row, 0, 0)
    tok = lambda w: pl.BlockSpec((None, tm, w), lambda bi, i: (bi, i, 0))
    heads = lambda n: pl.BlockSpec((None, n, tm, LANES), lambda bi, i: (bi, 0, i, 0))
    tab = pl.BlockSpec((tm, LANES), lambda bi, i: (i, 0))
    hshape = lambda n: jax.ShapeDtypeStruct((b, n, s, LANES), BF16)
    if kv_only:
        out_specs = [heads(4), heads(4), heads(2), heads(2)]
        out_shape = [hshape(4), hshape(4), hshape(2), hshape(2)]
    else:
        out_specs = [heads(4), heads(4), heads(4), heads(4), heads(2), heads(2),
                     tok(512), tok(512), tok(3 * d)]
        out_shape = [hshape(4), hshape(4), hshape(4), hshape(4), hshape(2), hshape(2),
                     jax.ShapeDtypeStruct((b, s, 512), F32),
                     jax.ShapeDtypeStruct((b, s, 512), F32),
                     jax.ShapeDtypeStruct((b, s, 3 * d), F32)]
    return pl.pallas_call(
        functools.partial(_in_kernel, use_rope=use_rope, kv_only=kv_only),
        grid=(b, s // tm),
        in_specs=[tok(d),
                  pl.BlockSpec((None, N_MOD, d), mod_map),
                  _resident((1, d)), _resident(w_in.shape),
                  _resident((1, LANES)), _resident((1, LANES)),
                  tab, tab, tab],
        out_specs=out_specs,
        out_shape=out_shape,
        scratch_shapes=[pltpu.VMEM((tm, d), BF16)],
        compiler_params=_params(2),
        name=("in_proj_rope" if use_rope else "in_proj_ctx") + ("_kv" if kv_only else ""),
    )(x, mods, g1, w_in, qg, kg, cos_t, s1_t, s2_t)


def _attn_kernel(*refs, mode, has_x, lam_init):
    if mode == "da":
        lam_ref, g_ref = refs[:2]
        refs = refs[2:]
    if has_x:
        q_ref, kc_ref, vc_ref, kx_ref, vx_ref, o_ref = refs[:6]
    else:
        q_ref, kc_ref, vc_ref, o_ref = refs[:4]
    if mode == "da":
        acc_ref = refs[-1]
    tq = q_ref.shape[0]
    rb = min(ATTN_ROWS, tq)
    low = lax.broadcasted_iota(jnp.int32, (rb, LANES), 1) < HEAD_DIM
    contract_last = (((1,), (1,)), ((), ()))
    if mode == "da":
        lv = lam_ref[...]
        lam = (jnp.exp(jnp.sum(lv[0:1] * lv[1:2], axis=-1, keepdims=True))
               - jnp.exp(jnp.sum(lv[2:3] * lv[3:4], axis=-1, keepdims=True)) + lam_init)

    def scores(r):
        q = q_ref[r * rb:(r + 1) * rb, :]
        zero = jnp.zeros_like(q)
        qm = jnp.concatenate([jnp.where(low, q, zero), jnp.where(low, zero, q)], axis=0)
        sc = lax.dot_general(qm, kc_ref[...], contract_last, preferred_element_type=F32)
        sx = None
        if has_x:
            sx = lax.dot_general(qm, kx_ref[...], contract_last, preferred_element_type=F32)
        return sc, sx

    def finish(r, sc, sx):
        m = jnp.max(sc, axis=-1, keepdims=True)
        if has_x:
            m = jnp.maximum(m, jnp.max(sx, axis=-1, keepdims=True))
        pc = jnp.exp(sc - m)
        den = jnp.sum(pc, axis=-1, keepdims=True)
        o = jnp.dot(pc.astype(BF16), vc_ref[...], preferred_element_type=F32)
        if has_x:
            px = jnp.exp(sx - m)
            den = den + jnp.sum(px, axis=-1, keepdims=True)
            o = o + jnp.dot(px.astype(BF16), vx_ref[...], preferred_element_type=F32)
        o = o / den
        o_lo, o_hi = o[:rb], o[rb:]
        if mode == "da":
            acc_ref[0, r * rb:(r + 1) * rb, :] = o_lo
            acc_ref[1, r * rb:(r + 1) * rb, :] = o_hi
        else:
            o_ref[r * rb:(r + 1) * rb, :] = jnp.where(low, o_lo, o_hi).astype(BF16)

    n_blocks = tq // rb
    pending = [scores(r) for r in range(min(ATTN_AHEAD, n_blocks))]
    for r in range(n_blocks):
        if r + ATTN_AHEAD < n_blocks:
            pending.append(scores(r + ATTN_AHEAD))
        finish(r, *pending.pop(0))
    if mode == "da":
        dlt = acc_ref[0] - lam * acc_ref[1]
        o_ref[...] = (_rms(dlt) * g_ref[...] * (1.0 - lam_init)).astype(BF16)


def _attn_stage(q, kc, vc, kx, vx, *, mode, lam_vec=None, subln_g=None, lam_init=0.0):
    b, nh, s, _ = q.shape
    tq = min(TQ, s)
    has_x = kx is not None
    if mode == "da":
        kv_head = lambda h: h
    else:
        kv_head = lambda h: h // 2
    q_spec = pl.BlockSpec((None, None, tq, LANES), lambda bi, h, i: (bi, h, i, 0))
    kv_spec = lambda n: pl.BlockSpec((None, None, n, LANES), lambda bi, h, i: (bi, kv_head(h), 0, 0))
    args, specs = [], []
    if mode == "da":
        args += [lam_vec, subln_g]
        specs += [pl.BlockSpec(lam_vec.shape, lambda bi, h, i: (0, 0)),
                  pl.BlockSpec(subln_g.shape, lambda bi, h, i: (0, 0))]
    args += [q, kc, vc]
    specs += [q_spec, kv_spec(kc.shape[2]), kv_spec(kc.shape[2])]
    if has_x:
        args += [kx, vx]
        specs += [kv_spec(kx.shape[2]), kv_spec(kx.shape[2])]
    return pl.pallas_call(
        functools.partial(_attn_kernel, mode=mode, has_x=has_x, lam_init=lam_init),
        grid=(b, nh, s // tq),
        in_specs=specs,
        out_specs=pl.BlockSpec((None, tq, LANES), lambda bi, h, i: (bi, i, h)),
        out_shape=jax.ShapeDtypeStruct((b, s, nh * LANES), BF16),
        scratch_shapes=[pltpu.VMEM((2, tq, LANES), F32)] if mode == "da" else [],
        compiler_params=_params(3),
        name=f"attn_{mode}_{'x' if has_x else 'ctx'}",
    )(*args)


def _post_kernel(*refs, final, n_tiles):
    (x_ref, ya_ref, yb_ref, cb_ref, cu_ref, cup_ref, cun_ref, gt_ref, mod_ref, g2_ref, cw_ref,
     wa_ref, wb_ref, wc_ref, wo_ref, wgu_ref, wd_ref) = refs[:17]
    if final:
        gf_ref, o_ref = refs[17:]
    else:
        (o_ref,) = refs[17:]
    tm = x_ref.shape[0]
    i = pl.program_id(1)

    cu = cu_ref[...]
    row = lax.broadcasted_iota(jnp.int32, cu.shape, 0)
    prev_row = jnp.where(i > 0, cup_ref[7:8, :], 0.0)
    next_row = jnp.where(i < n_tiles - 1, cun_ref[0:1, :], 0.0)
    before = jnp.where(row == 0, prev_row, pltpu.roll(cu, 1, 0))
    after = jnp.where(row == tm - 1, next_row, pltpu.roll(cu, tm - 1, 0))
    yc = cb_ref[...] * (before * cw_ref[0:1, :] + cu * cw_ref[1:2, :] + after * cw_ref[2:3, :])

    d = D_MODEL
    merged = (gt_ref[:, 0:d] * jnp.dot(ya_ref[...], wa_ref[...], preferred_element_type=F32)
              + gt_ref[:, d:2 * d] * jnp.dot(yb_ref[...], wb_ref[...], preferred_element_type=F32)
              + gt_ref[:, 2 * d:3 * d] * jnp.dot(yc.astype(BF16), wc_ref[...],
                                                 preferred_element_type=F32))
    mix = jnp.dot(merged.astype(BF16), wo_ref[...], preferred_element_type=F32)
    x1 = x_ref[...] + mod_ref[2:3, :] * mix

    h2 = (_rms(x1) * g2_ref[...] * (1.0 + mod_ref[4:5, :]) + mod_ref[3:4, :]).astype(BF16)
    gu = jnp.dot(h2, wgu_ref[...], preferred_element_type=F32)
    gate = gu[:, :D_FF]
    act = (gate * jax.nn.sigmoid(gate) * gu[:, D_FF:]).astype(BF16)
    x2 = x1 + mod_ref[5:6, :] * jnp.dot(act, wd_ref[...], preferred_element_type=F32)
    if final:
        x2 = _rms(x2) * gf_ref[...]
    o_ref[...] = x2


def _post_stage(x, ya, yb, cb, cu, gates, mods, mod_row, g2, conv_w, wa, wb, wc, wo, wgu, wd,
                final_g=None):
    b, s, d = x.shape
    tm = min(TM_POST, s)
    n_tiles = s // tm
    rows8 = tm // 8
    last8 = s // 8 - 1
    if mod_row is None:
        mod_map = lambda bi, i: (bi, 0, 0)
    else:
        mod_map = lambda bi, i: (mod_row, 0, 0)
    tok = lambda w: pl.BlockSpec((None, tm, w), lambda bi, i: (bi, i, 0))
    halo_prev = pl.BlockSpec((None, 8, 512), lambda bi, i: (bi, jnp.maximum(i * rows8 - 1, 0), 0))
    halo_next = pl.BlockSpec((None, 8, 512),
                             lambda bi, i: (bi, jnp.minimum((i + 1) * rows8, last8), 0))
    args = [x, ya, yb, cb, cu, cu, cu, gates, mods, g2, conv_w, wa, wb, wc, wo, wgu, wd]
    specs = [tok(d), tok(512), tok(512), tok(512), tok(512), halo_prev, halo_next, tok(3 * d),
             pl.BlockSpec((None, N_MOD, d), mod_map), _resident((1, d)), _resident(conv_w.shape),
             _resident(wa.shape), _resident(wb.shape), _resident(wc.shape), _resident(wo.shape),
             _resident(wgu.shape), _resident(wd.shape)]
    final = final_g is not None
    if final:
        args.append(final_g)
        specs.append(_resident((1, d)))
    return pl.pallas_call(
        functools.partial(_post_kernel, final=final, n_tiles=n_tiles),
        grid=(b, n_tiles),
        in_specs=specs,
        out_specs=tok(d),
        out_shape=jax.ShapeDtypeStruct((b, s, d), F32),
        compiler_params=_params(2),
        name="post_final" if final else "post",
    )(*args)


def _rope_tables(seq):
    rows = seq // GRID_W
    row = jnp.repeat(jnp.arange(rows, dtype=F32), GRID_W)
    col = jnp.tile(jnp.arange(GRID_W, dtype=F32), rows)
    n_freq = HEAD_DIM // 4
    inv_freq = ROPE_THETA ** (-jnp.arange(n_freq, dtype=F32) / n_freq)
    ang_r, ang_c = row[:, None] * inv_freq, col[:, None] * inv_freq
    cr, sr, cc, sc = jnp.cos(ang_r), jnp.sin(ang_r), jnp.cos(ang_c), jnp.sin(ang_c)
    z = jnp.zeros_like(sr)
    rep = lambda parts: jnp.tile(jnp.concatenate(parts, axis=1), (1, LANES // HEAD_DIM))
    return rep([cr, cr, cc, cc]), rep([-sr, z, -sc, z]), rep([z, sr, z, sc])


def kernel(x, c, ctx, c_ctx, w_mod, b_mod, norm1_g, norm2_g, w_in, lam_q1, lam_k1, lam_q2, lam_k2,
           diff_subln_g, q_norm_g, k_norm_g, conv_w, w_branch_a, w_branch_b, w_branch_c, w_out,
           w_ffn_gu, w_ffn_down, final_g):
    batch, seq, d = x.shape
    ctx_row = batch
    cc = jnp.concatenate([c, c_ctx[None, :], jnp.zeros((MOD_ROWS - batch - 1, d), F32)], axis=0)
    mods_all = _modulation(cc, w_mod, b_mod).reshape(DEPTH, MOD_ROWS, N_MOD, d)
    tables = _rope_tables(seq)
    row1 = lambda v: v.reshape(1, -1)
    pair = lambda v: jnp.tile(v, LANES // HEAD_DIM).reshape(1, LANES)

    for l in range(DEPTH):
        lam_init = 0.8 - 0.6 * math.exp(-0.3 * l)
        mods = mods_all[l]
        lam_vec = jnp.stack([lam_q1[l], lam_k1[l], lam_q2[l], lam_k2[l]]).astype(F32)
        w_in_l = w_in[l].astype(BF16)
        in_args = (row1(norm1_g[l]), w_in_l, pair(q_norm_g[l]), pair(k_norm_g[l]), tables)
        post_w = (row1(norm2_g[l]), conv_w[l], w_branch_a[l].astype(BF16),
                  w_branch_b[l].astype(BF16), w_branch_c[l].astype(BF16), w_out[l].astype(BF16),
                  w_ffn_gu[l].astype(BF16), w_ffn_down[l].astype(BF16))
        da = dict(mode="da", lam_vec=lam_vec, subln_g=row1(diff_subln_g[l]), lam_init=lam_init)

        last = l == DEPTH - 1
        if last:
            ka_c, va_c, kb_c, vb_c = _in_stage(ctx, mods, ctx_row, *in_args, use_rope=False,
                                               kv_only=True)
        else:
            qa_c, ka_c, va_c, qb_c, kb_c, vb_c, cb_c, cu_c, gt_c = _in_stage(
                ctx, mods, ctx_row, *in_args, use_rope=False)
        qa, ka, va, qb, kb, vb, cb, cu, gt = _in_stage(x, mods, None, *in_args, use_rope=True)

        ya = _attn_stage(qa, ka_c, va_c, ka, va, **da)
        yb = _attn_stage(qb, kb_c, vb_c, kb, vb, mode="gqa")
        x = _post_stage(x, ya, yb, cb, cu, gt, mods, None, *post_w,
                        final_g=row1(final_g) if last else None)
        if not last:
            ya_c = _attn_stage(qa_c, ka_c, va_c, None, None, **da)
            yb_c = _attn_stage(qb_c, kb_c, vb_c, None, None, mode="gqa")
            ctx = _post_stage(ctx, ya_c, yb_c, cb_c, cu_c, gt_c, mods, ctx_row, *post_w)
    return x
```

```python
import functools
import math

import jax
import jax.numpy as jnp
from jax import lax
from jax.experimental import pallas as pl
from jax.experimental.pallas import tpu as pltpu

F32 = jnp.float32
BF16 = jnp.bfloat16

D_MODEL = 1024
DEPTH = 2
GRID_W = 64
HEAD_DIM = 64
ROPE_THETA = 10000.0
NORM_EPS = 1e-6
DA_HEADS = 4
GQA_KV_HEADS = 2
N_MOD = 6
D_FF = 2816
LANES = 128
QK_SCALE = HEAD_DIM ** -0.5

C_AQ, C_AK, C_AV, C_BQ, C_BK, C_BV, C_CB, C_CC, C_CU, C_G, C_END = (
    0, 512, 1024, 1536, 2048, 2176, 2304, 2816, 3328, 3840, 6912)

MOD_ROWS = 24
V7X_VMEM_BYTES = 64 * 1024 * 1024
VMEM_LIMIT = V7X_VMEM_BYTES - 8 * 1024 * 1024
POST_VMEM_LIMIT = V7X_VMEM_BYTES - 4 * 1024 * 1024

TM_IN = 256
TQ = 2048
ATTN_ROWS = 64
TM_POST = 512


def _params(n_axes, vmem_limit=VMEM_LIMIT):
    return pltpu.CompilerParams(dimension_semantics=("arbitrary",) * n_axes,
                                vmem_limit_bytes=vmem_limit)


def _resident(shape):
    nd = len(shape)
    return pl.BlockSpec(shape, lambda *_: (0,) * nd, pipeline_mode=pl.Buffered(1))


def _mod_kernel(cc_ref, w_ref, b_ref, o_ref):
    a = cc_ref[...]
    a = (a * jax.nn.sigmoid(a)).astype(BF16)
    o_ref[...] = jnp.dot(a, w_ref[...].astype(BF16), preferred_element_type=F32) + b_ref[...]


def _modulation(cc, w_mod, b_mod):
    depth, d, n = w_mod.shape
    tn = 1024
    return pl.pallas_call(
        _mod_kernel,
        grid=(depth, n // tn),
        in_specs=[pl.BlockSpec((MOD_ROWS, d), lambda l, j: (0, 0)),
                  pl.BlockSpec((None, d, tn), lambda l, j: (l, 0, j)),
                  pl.BlockSpec((None, 1, tn), lambda l, j: (l, 0, j))],
        out_specs=pl.BlockSpec((None, MOD_ROWS, tn), lambda l, j: (l, 0, j)),
        out_shape=jax.ShapeDtypeStruct((depth, MOD_ROWS, n), F32),
        compiler_params=_params(2),
        name="modulation",
    )(cc, w_mod, b_mod.reshape(depth, 1, n))


def _rms(x):
    return x * lax.rsqrt(jnp.mean(x * x, axis=-1, keepdims=True) + NORM_EPS)


def _in_kernel(x_ref, mod_ref, g1_ref, w_ref, qg_ref, kg_ref, cos_ref, s1_ref, s2_ref,
               *refs, use_rope, kv_only):
    if kv_only:
        ka_ref, va_ref, kb_ref, vb_ref, h_ref = refs
    else:
        qa_ref, ka_ref, va_ref, qb_ref, kb_ref, vb_ref, cb_ref, cu_ref, gt_ref, h_ref = refs
    tm = x_ref.shape[0]
    y = _rms(x_ref[...]) * g1_ref[...]
    h_ref[...] = (y * (1.0 + mod_ref[1:2, :]) + mod_ref[0:1, :]).astype(BF16)

    def proj(c0, c1):
        return jnp.dot(h_ref[...], w_ref[:, c0:c1], preferred_element_type=F32)

    lane = lax.broadcasted_iota(jnp.int32, (tm, LANES), 1)
    low = lane < HEAD_DIM

    def rope(t):
        if not use_rope:
            return t
        return (t * cos_ref[...] + pltpu.roll(t, LANES - 16, 1) * s1_ref[...]
                + pltpu.roll(t, 16, 1) * s2_ref[...])

    def head_rms(t, g):
        tt = t * t
        s_lo = jnp.sum(jnp.where(low, tt, 0.0), axis=-1, keepdims=True)
        s_hi = jnp.sum(jnp.where(low, 0.0, tt), axis=-1, keepdims=True)
        ms = jnp.where(low, s_lo, s_hi) * (1.0 / HEAD_DIM)
        return t * lax.rsqrt(ms + NORM_EPS) * g

    def dup_halves(t):
        r = pltpu.roll(t, HEAD_DIM, 1)
        return jnp.where(low, t, r), jnp.where(low, r, t)

    if not kv_only:
        z = proj(C_AQ, C_AK)
        for hd in range(DA_HEADS):
            qa_ref[hd] = (rope(z[:, hd * LANES:(hd + 1) * LANES]) * QK_SCALE).astype(BF16)
    z = proj(C_AK, C_AV)
    for hd in range(DA_HEADS):
        ka_ref[hd] = rope(z[:, hd * LANES:(hd + 1) * LANES]).astype(BF16)
    if not kv_only:
        z = proj(C_BQ, C_BK)
        for p in range(4):
            t = head_rms(z[:, p * LANES:(p + 1) * LANES], qg_ref[...])
            qb_ref[p] = (rope(t) * QK_SCALE).astype(BF16)
    z = proj(C_BK, C_CB)
    k0, k1 = dup_halves(rope(head_rms(z[:, :LANES], kg_ref[...])))
    kb_ref[0] = k0.astype(BF16)
    kb_ref[1] = k1.astype(BF16)
    v01 = z[:, LANES:]
    vb_ref[0] = jnp.where(low, v01, 1.0).astype(BF16)
    vb_ref[1] = jnp.where(low, pltpu.roll(v01, HEAD_DIM, 1), 1.0).astype(BF16)
    if not kv_only:
        z = proj(C_CB, C_G)
        cb_ref[...] = z[:, :512]
        cu_ref[...] = z[:, 512:1024] * z[:, 1024:1536]
        for j in range(3):
            c0 = C_G + j * D_MODEL
            gt_ref[:, j * D_MODEL:(j + 1) * D_MODEL] = jax.nn.sigmoid(proj(c0, c0 + D_MODEL))
    z = proj(C_AV, C_BQ)
    for hd in range(DA_HEADS):
        va_ref[hd] = z[:, hd * LANES:(hd + 1) * LANES].astype(BF16)


def _in_stage(x, mods, mod_row, g1, w_in, qg, kg, tables, *, use_rope, kv_only=False):
    b, s, d = x.shape
    tm = min(TM_IN, s)
    cos_t, s1_t, s2_t = tables
    if mod_row is None:
        mod_map = lambda bi, i: (bi, 0, 0)
    else:
        mod_map = lambda bi, i: (mod_row, 0, 0)
    tok = lambda w: pl.BlockSpec((None, tm, w), lambda bi, i: (bi, i, 0))
    heads = lambda n: pl.BlockSpec((None, n, tm, LANES), lambda bi, i: (bi, 0, i, 0))
    tab = pl.BlockSpec((tm, LANES), lambda bi, i: (i, 0))
    hshape = lambda n: jax.ShapeDtypeStruct((b, n, s, LANES), BF16)
    if kv_only:
        out_specs = [heads(4), heads(4), heads(2), heads(2)]
        out_shape = [hshape(4), hshape(4), hshape(2), hshape(2)]
    else:
        out_specs = [heads(4), heads(4), heads(4), heads(4), heads(2), heads(2),
                     tok(512), tok(512), tok(3 * d)]
        out_shape = [hshape(4), hshape(4), hshape(4), hshape(4), hshape(2), hshape(2),
                     jax.ShapeDtypeStruct((b, s, 512), F32),
                     jax.ShapeDtypeStruct((b, s, 512), F32),
                     jax.ShapeDtypeStruct((b, s, 3 * d), F32)]
    return pl.pallas_call(
        functools.partial(_in_kernel, use_rope=use_rope, kv_only=kv_only),
        grid=(b, s // tm),
        in_specs=[tok(d),
                  pl.BlockSpec((None, N_MOD, d), mod_map),
                  _resident((1, d)), _resident(w_in.shape),
                  _resident((1, LANES)), _resident((1, LANES)),
                  tab, tab, tab],
        out_specs=out_specs,
        out_shape=out_shape,
        scratch_shapes=[pltpu.VMEM((tm, d), BF16)],
        compiler_params=_params(2),
        name=("in_proj_rope" if use_rope else "in_proj_ctx") + ("_kv" if kv_only else ""),
    )(x, mods, g1, w_in, qg, kg, cos_t, s1_t, s2_t)


def _attn_kernel(*refs, mode, has_x, lam_init):
    if mode == "da":
        lam_ref, g_ref = refs[:2]
        refs = refs[2:]
    if has_x:
        q_ref, kc_ref, vc_ref, kx_ref, vx_ref, o_ref = refs[:6]
    else:
        q_ref, kc_ref, vc_ref, o_ref = refs[:4]
    s_ref = refs[-1]
    if mode == "da":
        acc_ref = refs[-2]
    hb, tq, _ = q_ref.shape
    rb = s_ref.shape[1] // 2
    nc = kc_ref.shape[1]
    low = lax.broadcasted_iota(jnp.int32, (rb, LANES), 1) < HEAD_DIM
    contract_last = (((1,), (1,)), ((), ()))
    if mode == "da":
        lv = lam_ref[...]
        lam = (jnp.exp(jnp.sum(lv[0:1] * lv[1:2], axis=-1, keepdims=True))
               - jnp.exp(jnp.sum(lv[2:3] * lv[3:4], axis=-1, keepdims=True)) + lam_init)

    def scores(blk):
        hd, r, kv, slot = blk
        q = q_ref[hd, r * rb:(r + 1) * rb, :]
        zero = jnp.zeros_like(q)
        qm = jnp.concatenate([jnp.where(low, q, zero), jnp.where(low, zero, q)], axis=0)
        s_ref[slot, :, 0:nc] = lax.dot_general(qm, kc_ref[kv], contract_last,
                                               preferred_element_type=F32)
        if has_x:
            s_ref[slot, :, nc:] = lax.dot_general(qm, kx_ref[kv], contract_last,
                                                  preferred_element_type=F32)

    def finish(blk):
        hd, r, kv, slot = blk
        m = jnp.max(s_ref[slot], axis=-1, keepdims=True)
        pc = jnp.exp(s_ref[slot, :, 0:nc] - m)
        o = jnp.dot(pc.astype(BF16), vc_ref[kv], preferred_element_type=F32)
        if has_x:
            px = jnp.exp(s_ref[slot, :, nc:] - m)
            o = o + jnp.dot(px.astype(BF16), vx_ref[kv], preferred_element_type=F32)
        rows = slice(r * rb, (r + 1) * rb)
        if mode == "da":
            den = jnp.sum(pc, axis=-1, keepdims=True)
            if has_x:
                den = den + jnp.sum(px, axis=-1, keepdims=True)
            o = o / den
            acc_ref[0, rows, :] = o[:rb]
            acc_ref[1, rows, :] = o[rb:]
        else:
            o_lo, o_hi = o[:rb], o[rb:]
            res = jnp.where(low, o_lo / pltpu.roll(o_lo, HEAD_DIM, 1),
                            pltpu.roll(o_hi, HEAD_DIM, 1) / o_hi)
            o_ref[rows, hd * LANES:(hd + 1) * LANES] = res.astype(BF16)

    n_rows = tq // rb
    kv_of = (lambda hd: hd) if kc_ref.shape[0] == hb else (lambda hd: hd // 2)
    blocks = [(hd, r, kv_of(hd)) for hd in range(hb) for r in range(n_rows)]
    blocks = [(hd, r, kv, i % 2) for i, (hd, r, kv) in enumerate(blocks)]
    scores(blocks[0])
    for i, blk in enumerate(blocks):
        if i + 1 < len(blocks):
            scores(blocks[i + 1])
        finish(blk)
        hd, r = blk[0], blk[1]
        if mode == "da" and r == n_rows - 1:
            dlt = acc_ref[0] - lam * acc_ref[1]
            o_ref[:, hd * LANES:(hd + 1) * LANES] = (
                _rms(dlt) * g_ref[...] * (1.0 - lam_init)).astype(BF16)


def _attn_stage(q, kc, vc, kx, vx, *, mode, lam_vec=None, subln_g=None, lam_init=0.0):
    b, nh, s, _ = q.shape
    tq = min(TQ, s)
    rb = min(ATTN_ROWS, tq)
    has_x = kx is not None
    hb = nh if s * nh <= TQ else 1
    kvb = hb if mode == "da" else max(1, hb // 2)
    kv_block = (lambda h: h) if (mode == "da" or hb > 1) else (lambda h: h // 2)
    q_spec = pl.BlockSpec((None, hb, tq, LANES), lambda bi, h, i: (bi, h, i, 0))
    kv_spec = lambda n: pl.BlockSpec((None, kvb, n, LANES), lambda bi, h, i: (bi, kv_block(h), 0, 0))
    args, specs = [], []
    if mode == "da":
        args += [lam_vec, subln_g]
        specs += [pl.BlockSpec(lam_vec.shape, lambda bi, h, i: (0, 0)),
                  pl.BlockSpec(subln_g.shape, lambda bi, h, i: (0, 0))]
    args += [q, kc, vc]
    specs += [q_spec, kv_spec(kc.shape[2]), kv_spec(kc.shape[2])]
    n_keys = kc.shape[2]
    if has_x:
        args += [kx, vx]
        specs += [kv_spec(kx.shape[2]), kv_spec(kx.shape[2])]
        n_keys += kx.shape[2]
    scratch = [pltpu.VMEM((2, 2 * rb, n_keys), F32)]
    if mode == "da":
        scratch = [pltpu.VMEM((2, tq, LANES), F32)] + scratch
    return pl.pallas_call(
        functools.partial(_attn_kernel, mode=mode, has_x=has_x, lam_init=lam_init),
        grid=(b, nh // hb, s // tq),
        in_specs=specs,
        out_specs=pl.BlockSpec((None, tq, hb * LANES), lambda bi, h, i: (bi, i, h)),
        out_shape=jax.ShapeDtypeStruct((b, s, nh * LANES), BF16),
        scratch_shapes=scratch,
        compiler_params=_params(3),
        name=f"attn_{mode}_{'x' if has_x else 'ctx'}",
    )(*args)


def _post_kernel(*refs, final, n_tiles):
    (x_ref, ya_ref, yb_ref, cb_ref, cu_ref, cup_ref, cun_ref, gt_ref, mod_ref, g2_ref, cw_ref,
     wa_ref, wb_ref, wc_ref, wo_ref, wgu_ref, wd_ref) = refs[:17]
    if final:
        gf_ref, o_ref = refs[17:]
    else:
        (o_ref,) = refs[17:]
    tm = x_ref.shape[0]
    i = pl.program_id(1)

    cu = cu_ref[...]
    row = lax.broadcasted_iota(jnp.int32, cu.shape, 0)
    prev_row = jnp.where(i > 0, cup_ref[7:8, :], 0.0)
    next_row = jnp.where(i < n_tiles - 1, cun_ref[0:1, :], 0.0)
    before = jnp.where(row == 0, prev_row, pltpu.roll(cu, 1, 0))
    after = jnp.where(row == tm - 1, next_row, pltpu.roll(cu, tm - 1, 0))
    yc = cb_ref[...] * (before * cw_ref[0:1, :] + cu * cw_ref[1:2, :] + after * cw_ref[2:3, :])

    d = D_MODEL
    merged = (gt_ref[:, 0:d] * jnp.dot(ya_ref[...], wa_ref[...], preferred_element_type=F32)
              + gt_ref[:, d:2 * d] * jnp.dot(yb_ref[...], wb_ref[...], preferred_element_type=F32)
              + gt_ref[:, 2 * d:3 * d] * jnp.dot(yc.astype(BF16), wc_ref[...],
                                                 preferred_element_type=F32))
    mix = jnp.dot(merged.astype(BF16), wo_ref[...], preferred_element_type=F32)
    x1 = x_ref[...] + mod_ref[2:3, :] * mix

    h2 = (_rms(x1) * g2_ref[...] * (1.0 + mod_ref[4:5, :]) + mod_ref[3:4, :]).astype(BF16)
    gu = jnp.dot(h2, wgu_ref[...], preferred_element_type=F32)
    gate = gu[:, :D_FF]
    act = (gate * jax.nn.sigmoid(gate) * gu[:, D_FF:]).astype(BF16)
    x2 = x1 + mod_ref[5:6, :] * jnp.dot(act, wd_ref[...], preferred_element_type=F32)
    if final:
        x2 = _rms(x2) * gf_ref[...]
    o_ref[...] = x2


def _post_stage(x, ya, yb, cb, cu, gates, mods, mod_row, g2, conv_w, wa, wb, wc, wo, wgu, wd,
                final_g=None):
    b, s, d = x.shape
    tm = min(TM_POST, s)
    n_tiles = s // tm
    rows8 = tm // 8
    last8 = s // 8 - 1
    if mod_row is None:
        mod_map = lambda bi, i: (bi, 0, 0)
    else:
        mod_map = lambda bi, i: (mod_row, 0, 0)
    tok = lambda w: pl.BlockSpec((None, tm, w), lambda bi, i: (bi, i, 0))
    halo_prev = pl.BlockSpec((None, 8, 512), lambda bi, i: (bi, jnp.maximum(i * rows8 - 1, 0), 0))
    halo_next = pl.BlockSpec((None, 8, 512),
                             lambda bi, i: (bi, jnp.minimum((i + 1) * rows8, last8), 0))
    args = [x, ya, yb, cb, cu, cu, cu, gates, mods, g2, conv_w, wa, wb, wc, wo, wgu, wd]
    specs = [tok(d), tok(512), tok(512), tok(512), tok(512), halo_prev, halo_next, tok(3 * d),
             pl.BlockSpec((None, N_MOD, d), mod_map), _resident((1, d)), _resident(conv_w.shape),
             _resident(wa.shape), _resident(wb.shape), _resident(wc.shape), _resident(wo.shape),
             _resident(wgu.shape), _resident(wd.shape)]
    final = final_g is not None
    if final:
        args.append(final_g)
        specs.append(_resident((1, d)))
    return pl.pallas_call(
        functools.partial(_post_kernel, final=final, n_tiles=n_tiles),
        grid=(b, n_tiles),
        in_specs=specs,
        out_specs=tok(d),
        out_shape=jax.ShapeDtypeStruct((b, s, d), F32),
        compiler_params=_params(2, POST_VMEM_LIMIT),
        name="post_final" if final else "post",
    )(*args)


def _rope_tables(seq):
    rows = seq // GRID_W
    row = jnp.repeat(jnp.arange(rows, dtype=F32), GRID_W)
    col = jnp.tile(jnp.arange(GRID_W, dtype=F32), rows)
    n_freq = HEAD_DIM // 4
    inv_freq = ROPE_THETA ** (-jnp.arange(n_freq, dtype=F32) / n_freq)
    ang_r, ang_c = row[:, None] * inv_freq, col[:, None] * inv_freq
    cr, sr, cc, sc = jnp.cos(ang_r), jnp.sin(ang_r), jnp.cos(ang_c), jnp.sin(ang_c)
    z = jnp.zeros_like(sr)
    rep = lambda parts: jnp.tile(jnp.concatenate(parts, axis=1), (1, LANES // HEAD_DIM))
    return rep([cr, cr, cc, cc]), rep([-sr, z, -sc, z]), rep([z, sr, z, sc])


def kernel(x, c, ctx, c_ctx, w_mod, b_mod, norm1_g, norm2_g, w_in, lam_q1, lam_k1, lam_q2, lam_k2,
           diff_subln_g, q_norm_g, k_norm_g, conv_w, w_branch_a, w_branch_b, w_branch_c, w_out,
           w_ffn_gu, w_ffn_down, final_g):
    batch, seq, d = x.shape
    ctx_row = batch
    cc = jnp.concatenate([c, c_ctx[None, :], jnp.zeros((MOD_ROWS - batch - 1, d), F32)], axis=0)
    mods_all = _modulation(cc, w_mod, b_mod).reshape(DEPTH, MOD_ROWS, N_MOD, d)
    tables = _rope_tables(seq)
    row1 = lambda v: v.reshape(1, -1)
    pair = lambda v: jnp.tile(v, LANES // HEAD_DIM).reshape(1, LANES)

    for l in range(DEPTH):
        lam_init = 0.8 - 0.6 * math.exp(-0.3 * l)
        mods = mods_all[l]
        lam_vec = jnp.stack([lam_q1[l], lam_k1[l], lam_q2[l], lam_k2[l]]).astype(F32)
        w_in_l = w_in[l].astype(BF16)
        in_args = (row1(norm1_g[l]), w_in_l, pair(q_norm_g[l]), pair(k_norm_g[l]), tables)
        post_w = (row1(norm2_g[l]), conv_w[l], w_branch_a[l].astype(BF16),
                  w_branch_b[l].astype(BF16), w_branch_c[l].astype(BF16), w_out[l].astype(BF16),
                  w_ffn_gu[l].astype(BF16), w_ffn_down[l].astype(BF16))
        da = dict(mode="da", lam_vec=lam_vec, subln_g=row1(diff_subln_g[l]), lam_init=lam_init)

        last = l == DEPTH - 1
        if last:
            ka_c, va_c, kb_c, vb_c = _in_stage(ctx, mods, ctx_row, *in_args, use_rope=False,
                                               kv_only=True)
        else:
            qa_c, ka_c, va_c, qb_c, kb_c, vb_c, cb_c, cu_c, gt_c = _in_stage(
                ctx, mods, ctx_row, *in_args, use_rope=False)
        qa, ka, va, qb, kb, vb, cb, cu, gt = _in_stage(x, mods, None, *in_args, use_rope=True)

        ya = _attn_stage(qa, ka_c, va_c, ka, va, **da)
        yb = _attn_stage(qb, kb_c, vb_c, kb, vb, mode="gqa")
        x = _post_stage(x, ya, yb, cb, cu, gt, mods, None, *post_w,
                        final_g=row1(final_g) if last else None)
        if not last:
            ya_c = _attn_stage(qa_c, ka_c, va_c, None, None, **da)
            yb_c = _attn_stage(qb_c, kb_c, vb_c, None, None, mode="gqa")
            ctx = _post_stage(ctx, ya_c, yb_c, cb_c, cu_c, gt_c, mods, ctx_row, *post_w)
    return x
```

```python
import functools
import math

import jax
import jax.numpy as jnp
from jax import lax
from jax.experimental import pallas as pl
from jax.experimental.pallas import tpu as pltpu

F32 = jnp.float32
BF16 = jnp.bfloat16

D_MODEL = 1024
DEPTH = 2
GRID_W = 64
HEAD_DIM = 64
ROPE_THETA = 10000.0
NORM_EPS = 1e-6
DA_HEADS = 4
GQA_KV_HEADS = 2
N_MOD = 6
D_FF = 2816
LANES = 128
QK_SCALE = HEAD_DIM ** -0.5

C_AQ, C_AK, C_AV, C_BQ, C_BK, C_BV, C_CB, C_CC, C_CU, C_G, C_END = (
    0, 512, 1024, 1536, 2048, 2176, 2304, 2816, 3328, 3840, 6912)

MOD_ROWS = 24
V7X_VMEM_BYTES = 64 * 1024 * 1024
VMEM_LIMIT = V7X_VMEM_BYTES - 8 * 1024 * 1024
POST_VMEM_LIMIT = V7X_VMEM_BYTES - 4 * 1024 * 1024

TM_IN = 512
IN_CHAINS = 2
TQ = 2048
ATTN_ROWS = {"da": 64, "gqa": 128}
TM_POST = 512
POST_CHAINS = 2


def _params(n_axes, vmem_limit=VMEM_LIMIT):
    return pltpu.CompilerParams(dimension_semantics=("arbitrary",) * n_axes,
                                vmem_limit_bytes=vmem_limit)


def _resident(shape):
    nd = len(shape)
    return pl.BlockSpec(shape, lambda *_: (0,) * nd, pipeline_mode=pl.Buffered(1))


def _resident_layer(stacked, layer):
    nd = stacked.ndim - 1
    return pl.BlockSpec((None,) + stacked.shape[1:], lambda *_: (layer,) + (0,) * nd,
                        pipeline_mode=pl.Buffered(1))


def _mod_kernel(cc_ref, w_ref, b_ref, o_ref):
    a = cc_ref[...]
    a = (a * jax.nn.sigmoid(a)).astype(BF16)
    o_ref[...] = jnp.dot(a, w_ref[...].astype(BF16), preferred_element_type=F32) + b_ref[...]


def _modulation(cc, w_mod, b_mod):
    depth, d, n = w_mod.shape
    tn = 1024
    return pl.pallas_call(
        _mod_kernel,
        grid=(depth, n // tn),
        in_specs=[pl.BlockSpec((MOD_ROWS, d), lambda l, j: (0, 0)),
                  pl.BlockSpec((None, d, tn), lambda l, j: (l, 0, j)),
                  pl.BlockSpec((None, 1, tn), lambda l, j: (l, 0, j))],
        out_specs=pl.BlockSpec((None, MOD_ROWS, tn), lambda l, j: (l, 0, j)),
        out_shape=jax.ShapeDtypeStruct((depth, MOD_ROWS, n), F32),
        compiler_params=_params(2),
        name="modulation",
    )(cc, w_mod, b_mod.reshape(depth, 1, n))


def _rms(x):
    return x * lax.rsqrt(jnp.mean(x * x, axis=-1, keepdims=True) + NORM_EPS)


def _in_kernel(x_ref, mod_ref, g1_ref, w_ref, qg_ref, kg_ref, cos_ref, s1_ref, s2_ref,
               *refs, use_rope, kv_only):
    if kv_only:
        ka_ref, va_ref, kb_ref, vb_ref, h_ref = refs
    else:
        qa_ref, ka_ref, va_ref, qb_ref, kb_ref, vb_ref, cb_ref, cu_ref, gt_ref, h_ref = refs
    tm = x_ref.shape[0]
    nch = IN_CHAINS if tm % (256 * IN_CHAINS) == 0 else 1
    rc = tm // nch
    chains = [slice(c * rc, (c + 1) * rc) for c in range(nch)]
    for rs in chains:
        y = _rms(x_ref[rs, :]) * g1_ref[...]
        h_ref[rs, :] = (y * (1.0 + mod_ref[1:2, :]) + mod_ref[0:1, :]).astype(BF16)

    def proj(rs, c0, c1):
        return jnp.dot(h_ref[rs, :], w_ref[:, c0:c1], preferred_element_type=F32)

    low = lax.broadcasted_iota(jnp.int32, (rc, LANES), 1) < HEAD_DIM

    def rope(rs, t):
        if not use_rope:
            return t
        return (t * cos_ref[rs, :] + pltpu.roll(t, LANES - 16, 1) * s1_ref[rs, :]
                + pltpu.roll(t, 16, 1) * s2_ref[rs, :])

    def head_rms(t, g):
        tt = t * t
        s_lo = jnp.sum(jnp.where(low, tt, 0.0), axis=-1, keepdims=True)
        s_hi = jnp.sum(jnp.where(low, 0.0, tt), axis=-1, keepdims=True)
        ms = jnp.where(low, s_lo, s_hi) * (1.0 / HEAD_DIM)
        return t * lax.rsqrt(ms + NORM_EPS) * g

    def dup_halves(t):
        r = pltpu.roll(t, HEAD_DIM, 1)
        return jnp.where(low, t, r), jnp.where(low, r, t)

    group = lambda hd: slice(hd * LANES, (hd + 1) * LANES)
    if not kv_only:
        for rs in chains:
            z = proj(rs, C_AQ, C_AK)
            for hd in range(DA_HEADS):
                qa_ref[hd, rs, :] = (rope(rs, z[:, group(hd)]) * QK_SCALE).astype(BF16)
    for rs in chains:
        z = proj(rs, C_AK, C_AV)
        for hd in range(DA_HEADS):
            ka_ref[hd, rs, :] = rope(rs, z[:, group(hd)]).astype(BF16)
    if not kv_only:
        for rs in chains:
            z = proj(rs, C_BQ, C_BK)
            for p in range(4):
                t = head_rms(z[:, group(p)], qg_ref[...])
                qb_ref[p, rs, :] = (rope(rs, t) * QK_SCALE).astype(BF16)
    for rs in chains:
        z = proj(rs, C_BK, C_CB)
        k0, k1 = dup_halves(rope(rs, head_rms(z[:, :LANES], kg_ref[...])))
        kb_ref[0, rs, :] = k0.astype(BF16)
        kb_ref[1, rs, :] = k1.astype(BF16)
        v01 = z[:, LANES:]
        vb_ref[0, rs, :] = jnp.where(low, v01, 1.0).astype(BF16)
        vb_ref[1, rs, :] = jnp.where(low, pltpu.roll(v01, HEAD_DIM, 1), 1.0).astype(BF16)
    if not kv_only:
        for rs in chains:
            z = proj(rs, C_CB, C_G)
            cb_ref[rs, :] = z[:, :512]
            cu_ref[rs, :] = z[:, 512:1024] * z[:, 1024:1536]
        for j in range(3):
            c0 = C_G + j * D_MODEL
            for rs in chains:
                gt_ref[rs, j * D_MODEL:(j + 1) * D_MODEL] = jax.nn.sigmoid(
                    proj(rs, c0, c0 + D_MODEL))
    for rs in chains:
        z = proj(rs, C_AV, C_BQ)
        for hd in range(DA_HEADS):
            va_ref[hd, rs, :] = z[:, group(hd)].astype(BF16)


def _in_stage(x, mods, mod_row, layer, g1, w_in, qg, kg, tables, *, use_rope, kv_only=False):
    b, s, d = x.shape
    tm = min(TM_IN, s)
    cos_t, s1_t, s2_t = tables
    if mod_row is None:
        mod_map = lambda bi, i: (bi, 0, 0)
    else:
        mod_map = lambda bi, i: (mod_row, 0, 0)
    tok = lambda w: pl.BlockSpec((None, tm, w), lambda bi, i: (bi, i, 0))
    heads = lambda n: pl.BlockSpec((None, n, tm, LANES), lambda bi, i: (bi, 0, i, 0))
    tab = pl.BlockSpec((tm, LANES), lambda bi, i: (i, 0))
    hshape = lambda n: jax.ShapeDtypeStruct((b, n, s, LANES), BF16)
    if kv_only:
        out_specs = [heads(4), heads(4), heads(2), heads(2)]
        out_shape = [hshape(4), hshape(4), hshape(2), hshape(2)]
    else:
        out_specs = [heads(4), heads(4), heads(4), heads(4), heads(2), heads(2),
                     tok(512), tok(512), tok(3 * d)]
        out_shape = [hshape(4), hshape(4), hshape(4), hshape(4), hshape(2), hshape(2),
                     jax.ShapeDtypeStruct((b, s, 512), F32),
                     jax.ShapeDtypeStruct((b, s, 512), F32),
                     jax.ShapeDtypeStruct((b, s, 3 * d), F32)]
    return pl.pallas_call(
        functools.partial(_in_kernel, use_rope=use_rope, kv_only=kv_only),
        grid=(b, s // tm),
        in_specs=[tok(d),
                  pl.BlockSpec((None, N_MOD, d), mod_map),
                  _resident((1, d)), _resident_layer(w_in, layer),
                  _resident((1, LANES)), _resident((1, LANES)),
                  tab, tab, tab],
        out_specs=out_specs,
        out_shape=out_shape,
        scratch_shapes=[pltpu.VMEM((tm, d), BF16)],
        compiler_params=_params(2),
        name=("in_proj_rope" if use_rope else "in_proj_ctx") + ("_kv" if kv_only else ""),
    )(x, mods, g1, w_in, qg, kg, cos_t, s1_t, s2_t)


def _attn_kernel(*refs, mode, has_x, lam_init):
    if mode == "da":
        lam_ref, g_ref = refs[:2]
        refs = refs[2:]
    if has_x:
        q_ref, kc_ref, vc_ref, kx_ref, vx_ref, o_ref = refs[:6]
    else:
        q_ref, kc_ref, vc_ref, o_ref = refs[:4]
    s_ref = refs[-1]
    if mode == "da":
        acc_ref = refs[-2]
    hb, tq, _ = q_ref.shape
    rb = s_ref.shape[1] // 2
    nc = kc_ref.shape[1]
    low = lax.broadcasted_iota(jnp.int32, (rb, LANES), 1) < HEAD_DIM
    contract_last = (((1,), (1,)), ((), ()))
    if mode == "da":
        lv = lam_ref[...]
        lam = (jnp.exp(jnp.sum(lv[0:1] * lv[1:2], axis=-1, keepdims=True))
               - jnp.exp(jnp.sum(lv[2:3] * lv[3:4], axis=-1, keepdims=True)) + lam_init)

    def scores(blk):
        hd, r, kv, slot = blk
        q = q_ref[hd, r * rb:(r + 1) * rb, :]
        zero = jnp.zeros_like(q)
        qm = jnp.concatenate([jnp.where(low, q, zero), jnp.where(low, zero, q)], axis=0)
        s_ref[slot, :, 0:nc] = lax.dot_general(qm, kc_ref[kv], contract_last,
                                               preferred_element_type=F32)
        if has_x:
            s_ref[slot, :, nc:] = lax.dot_general(qm, kx_ref[kv], contract_last,
                                                  preferred_element_type=F32)

    def finish(blk):
        hd, r, kv, slot = blk
        m = jnp.max(s_ref[slot], axis=-1, keepdims=True)
        pc = jnp.exp(s_ref[slot, :, 0:nc] - m)
        o = jnp.dot(pc.astype(BF16), vc_ref[kv], preferred_element_type=F32)
        if has_x:
            px = jnp.exp(s_ref[slot, :, nc:] - m)
            o = o + jnp.dot(px.astype(BF16), vx_ref[kv], preferred_element_type=F32)
        rows = slice(r * rb, (r + 1) * rb)
        if mode == "da":
            den = jnp.sum(pc, axis=-1, keepdims=True)
            if has_x:
                den = den + jnp.sum(px, axis=-1, keepdims=True)
            o = o / den
            acc_ref[0, rows, :] = o[:rb]
            acc_ref[1, rows, :] = o[rb:]
        else:
            o_lo, o_hi = o[:rb], o[rb:]
            res = jnp.where(low, o_lo / pltpu.roll(o_lo, HEAD_DIM, 1),
                            pltpu.roll(o_hi, HEAD_DIM, 1) / o_hi)
            o_ref[rows, hd * LANES:(hd + 1) * LANES] = res.astype(BF16)

    n_rows = tq // rb
    kv_of = (lambda hd: hd) if kc_ref.shape[0] == hb else (lambda hd: hd // 2)
    blocks = [(hd, r, kv_of(hd)) for hd in range(hb) for r in range(n_rows)]
    blocks = [(hd, r, kv, i % 2) for i, (hd, r, kv) in enumerate(blocks)]
    scores(blocks[0])
    for i, blk in enumerate(blocks):
        if i + 1 < len(blocks):
            scores(blocks[i + 1])
        finish(blk)
        hd, r = blk[0], blk[1]
        if mode == "da" and r == n_rows - 1:
            dlt = acc_ref[0] - lam * acc_ref[1]
            o_ref[:, hd * LANES:(hd + 1) * LANES] = (
                _rms(dlt) * g_ref[...] * (1.0 - lam_init)).astype(BF16)


def _attn_stage(q, kc, vc, kx, vx, *, mode, lam_vec=None, subln_g=None, lam_init=0.0):
    b, nh, s, _ = q.shape
    tq = min(TQ, s)
    rb = min(ATTN_ROWS[mode], tq)
    has_x = kx is not None
    hb = nh if s * nh <= TQ else 1
    kvb = hb if mode == "da" else max(1, hb // 2)
    kv_block = (lambda h: h) if (mode == "da" or hb > 1) else (lambda h: h // 2)
    q_spec = pl.BlockSpec((None, hb, tq, LANES), lambda bi, h, i: (bi, h, i, 0))
    kv_spec = lambda n: pl.BlockSpec((None, kvb, n, LANES), lambda bi, h, i: (bi, kv_block(h), 0, 0))
    args, specs = [], []
    if mode == "da":
        args += [lam_vec, subln_g]
        specs += [pl.BlockSpec(lam_vec.shape, lambda bi, h, i: (0, 0)),
                  pl.BlockSpec(subln_g.shape, lambda bi, h, i: (0, 0))]
    args += [q, kc, vc]
    specs += [q_spec, kv_spec(kc.shape[2]), kv_spec(kc.shape[2])]
    n_keys = kc.shape[2]
    if has_x:
        args += [kx, vx]
        specs += [kv_spec(kx.shape[2]), kv_spec(kx.shape[2])]
        n_keys += kx.shape[2]
    scratch = [pltpu.VMEM((2, 2 * rb, n_keys), F32)]
    if mode == "da":
        scratch = [pltpu.VMEM((2, tq, LANES), F32)] + scratch
    return pl.pallas_call(
        functools.partial(_attn_kernel, mode=mode, has_x=has_x, lam_init=lam_init),
        grid=(b, nh // hb, s // tq),
        in_specs=specs,
        out_specs=pl.BlockSpec((None, tq, hb * LANES), lambda bi, h, i: (bi, i, h)),
        out_shape=jax.ShapeDtypeStruct((b, s, nh * LANES), BF16),
        scratch_shapes=scratch,
        compiler_params=_params(3),
        name=f"attn_{mode}_{'x' if has_x else 'ctx'}",
    )(*args)


def _post_kernel(*refs, final, n_tiles):
    (x_ref, ya_ref, yb_ref, cb_ref, cu_ref, cup_ref, cun_ref, gt_ref, mod_ref, g2_ref, cw_ref,
     wa_ref, wb_ref, wc_ref, wo_ref, wgu_ref, wd_ref) = refs[:17]
    if final:
        gf_ref, o_ref = refs[17:]
    else:
        (o_ref,) = refs[17:]
    tm = x_ref.shape[0]
    i = pl.program_id(1)

    cu = cu_ref[...]
    row = lax.broadcasted_iota(jnp.int32, cu.shape, 0)
    prev_row = jnp.where(i > 0, cup_ref[7:8, :], 0.0)
    next_row = jnp.where(i < n_tiles - 1, cun_ref[0:1, :], 0.0)
    before = jnp.where(row == 0, prev_row, pltpu.roll(cu, 1, 0))
    after = jnp.where(row == tm - 1, next_row, pltpu.roll(cu, tm - 1, 0))
    yc = (cb_ref[...] * (before * cw_ref[0:1, :] + cu * cw_ref[1:2, :] + after * cw_ref[2:3, :])
          ).astype(BF16)

    d = D_MODEL
    nch = POST_CHAINS if tm % (128 * POST_CHAINS) == 0 else 1
    rc = tm // nch
    chains = [slice(c * rc, (c + 1) * rc) for c in range(nch)]
    dot = functools.partial(jnp.dot, preferred_element_type=F32)
    merged = [(gt_ref[rs, 0:d] * dot(ya_ref[rs, :], wa_ref[...])
               + gt_ref[rs, d:2 * d] * dot(yb_ref[rs, :], wb_ref[...])
               + gt_ref[rs, 2 * d:3 * d] * dot(yc[rs], wc_ref[...])).astype(BF16) for rs in chains]
    x1 = [x_ref[rs, :] + mod_ref[2:3, :] * dot(mg, wo_ref[...]) for rs, mg in zip(chains, merged)]
    h2 = [(_rms(v) * g2_ref[...] * (1.0 + mod_ref[4:5, :]) + mod_ref[3:4, :]).astype(BF16)
          for v in x1]
    act = []
    for h in h2:
        gu = dot(h, wgu_ref[...])
        gate = gu[:, :D_FF]
        act.append((gate * jax.nn.sigmoid(gate) * gu[:, D_FF:]).astype(BF16))
    for rs, v, a in zip(chains, x1, act):
        x2 = v + mod_ref[5:6, :] * dot(a, wd_ref[...])
        if final:
            x2 = _rms(x2) * gf_ref[...]
        o_ref[rs, :] = x2


def _post_stage(x, ya, yb, cb, cu, gates, mods, mod_row, layer, g2, conv_w, wa, wb, wc, wo, wgu,
                wd, final_g=None):
    b, s, d = x.shape
    tm = min(TM_POST, s)
    n_tiles = s // tm
    rows8 = tm // 8
    last8 = s // 8 - 1
    if mod_row is None:
        mod_map = lambda bi, i: (bi, 0, 0)
    else:
        mod_map = lambda bi, i: (mod_row, 0, 0)
    tok = lambda w: pl.BlockSpec((None, tm, w), lambda bi, i: (bi, i, 0))
    halo_prev = pl.BlockSpec((None, 8, 512), lambda bi, i: (bi, jnp.maximum(i * rows8 - 1, 0), 0))
    halo_next = pl.BlockSpec((None, 8, 512),
                             lambda bi, i: (bi, jnp.minimum((i + 1) * rows8, last8), 0))
    args = [x, ya, yb, cb, cu, cu, cu, gates, mods, g2, conv_w, wa, wb, wc, wo, wgu, wd]
    specs = [tok(d), tok(512), tok(512), tok(512), tok(512), halo_prev, halo_next, tok(3 * d),
             pl.BlockSpec((None, N_MOD, d), mod_map), _resident((1, d)), _resident(conv_w.shape)]
    specs += [_resident_layer(w, layer) for w in (wa, wb, wc, wo, wgu, wd)]
    final = final_g is not None
    if final:
        args.append(final_g)
        specs.append(_resident((1, d)))
    return pl.pallas_call(
        functools.partial(_post_kernel, final=final, n_tiles=n_tiles),
        grid=(b, n_tiles),
        in_specs=specs,
        out_specs=tok(d),
        out_shape=jax.ShapeDtypeStruct((b, s, d), F32),
        compiler_params=_params(2, POST_VMEM_LIMIT),
        name="post_final" if final else "post",
    )(*args)


def _rope_tables(seq):
    rows = seq // GRID_W
    row = jnp.repeat(jnp.arange(rows, dtype=F32), GRID_W)
    col = jnp.tile(jnp.arange(GRID_W, dtype=F32), rows)
    n_freq = HEAD_DIM // 4
    inv_freq = ROPE_THETA ** (-jnp.arange(n_freq, dtype=F32) / n_freq)
    ang_r, ang_c = row[:, None] * inv_freq, col[:, None] * inv_freq
    cr, sr, cc, sc = jnp.cos(ang_r), jnp.sin(ang_r), jnp.cos(ang_c), jnp.sin(ang_c)
    z = jnp.zeros_like(sr)
    rep = lambda parts: jnp.tile(jnp.concatenate(parts, axis=1), (1, LANES // HEAD_DIM))
    return rep([cr, cr, cc, cc]), rep([-sr, z, -sc, z]), rep([z, sr, z, sc])


def kernel(x, c, ctx, c_ctx, w_mod, b_mod, norm1_g, norm2_g, w_in, lam_q1, lam_k1, lam_q2, lam_k2,
           diff_subln_g, q_norm_g, k_norm_g, conv_w, w_branch_a, w_branch_b, w_branch_c, w_out,
           w_ffn_gu, w_ffn_down, final_g):
    batch, seq, d = x.shape
    ctx_row = batch
    cc = jnp.concatenate([c, c_ctx[None, :], jnp.zeros((MOD_ROWS - batch - 1, d), F32)], axis=0)
    mods_all = _modulation(cc, w_mod, b_mod).reshape(DEPTH, MOD_ROWS, N_MOD, d)
    tables = _rope_tables(seq)
    row1 = lambda v: v.reshape(1, -1)
    pair = lambda v: jnp.tile(v, LANES // HEAD_DIM).reshape(1, LANES)
    w_in_bf = w_in.astype(BF16)
    post_stacks = tuple(w.astype(BF16) for w in (w_branch_a, w_branch_b, w_branch_c, w_out,
                                                 w_ffn_gu, w_ffn_down))

    for l in range(DEPTH):
        lam_init = 0.8 - 0.6 * math.exp(-0.3 * l)
        mods = mods_all[l]
        lam_vec = jnp.stack([lam_q1[l], lam_k1[l], lam_q2[l], lam_k2[l]]).astype(F32)
        in_args = (l, row1(norm1_g[l]), w_in_bf, pair(q_norm_g[l]), pair(k_norm_g[l]), tables)
        post_w = (l, row1(norm2_g[l]), conv_w[l]) + post_stacks
        da = dict(mode="da", lam_vec=lam_vec, subln_g=row1(diff_subln_g[l]), lam_init=lam_init)

        last = l == DEPTH - 1
        if last:
            ka_c, va_c, kb_c, vb_c = _in_stage(ctx, mods, ctx_row, *in_args, use_rope=False,
                                               kv_only=True)
        else:
            qa_c, ka_c, va_c, qb_c, kb_c, vb_c, cb_c, cu_c, gt_c = _in_stage(
                ctx, mods, ctx_row, *in_args, use_rope=False)
        qa, ka, va, qb, kb, vb, cb, cu, gt = _in_stage(x, mods, None, *in_args, use_rope=True)

        ya = _attn_stage(qa, ka_c, va_c, ka, va, **da)
        yb = _attn_stage(qb, kb_c, vb_c, kb, vb, mode="gqa")
        x = _post_stage(x, ya, yb, cb, cu, gt, mods, None, *post_w,
                        final_g=row1(final_g) if last else None)
        if not last:
            ya_c = _attn_stage(qa_c, ka_c, va_c, None, None, **da)
            yb_c = _attn_stage(qb_c, kb_c, vb_c, None, None, mode="gqa")
            ctx = _post_stage(ctx, ya_c, yb_c, cb_c, cu_c, gt_c, mods, ctx_row, *post_w)
    return x
```

```python
import functools
import math

import jax
import jax.numpy as jnp
from jax import lax
from jax.experimental import pallas as pl
from jax.experimental.pallas import tpu as pltpu

F32 = jnp.float32
BF16 = jnp.bfloat16

D_MODEL = 1024
DEPTH = 2
GRID_W = 64
HEAD_DIM = 64
ROPE_THETA = 10000.0
NORM_EPS = 1e-6
DA_HEADS = 4
GQA_KV_HEADS = 2
N_MOD = 6
D_FF = 2816
LANES = 128
QK_SCALE = HEAD_DIM ** -0.5 * math.log2(math.e)

C_AQ, C_AK, C_AV, C_BQ, C_BK, C_BV, C_CB, C_CC, C_CU, C_G, C_END = (
    0, 512, 1024, 1536, 2048, 2176, 2304, 2816, 3328, 3840, 6912)

MOD_ROWS = 24
V7X_VMEM_BYTES = 64 * 1024 * 1024
VMEM_LIMIT = V7X_VMEM_BYTES - 8 * 1024 * 1024
POST_VMEM_LIMIT = V7X_VMEM_BYTES - 4 * 1024 * 1024

TM_IN = 512
IN_CHAINS = 2
TQ = 2048
ATTN_ROWS = {"da": 64, "gqa": 128}
TM_POST = 512
POST_CHAINS = 2


def _params(n_axes, vmem_limit=VMEM_LIMIT):
    return pltpu.CompilerParams(dimension_semantics=("arbitrary",) * n_axes,
                                vmem_limit_bytes=vmem_limit)


def _resident(shape):
    nd = len(shape)
    return pl.BlockSpec(shape, lambda *_: (0,) * nd, pipeline_mode=pl.Buffered(1))


def _resident_layer(stacked, layer):
    nd = stacked.ndim - 1
    return pl.BlockSpec((None,) + stacked.shape[1:], lambda *_: (layer,) + (0,) * nd,
                        pipeline_mode=pl.Buffered(1))


def _mod_kernel(cc_ref, w_ref, b_ref, o_ref):
    a = cc_ref[...]
    a = (a * jax.nn.sigmoid(a)).astype(BF16)
    o_ref[...] = jnp.dot(a, w_ref[...].astype(BF16), preferred_element_type=F32) + b_ref[...]


def _modulation(cc, w_mod, b_mod):
    depth, d, n = w_mod.shape
    tn = 1024
    return pl.pallas_call(
        _mod_kernel,
        grid=(depth, n // tn),
        in_specs=[pl.BlockSpec((MOD_ROWS, d), lambda l, j: (0, 0)),
                  pl.BlockSpec((None, d, tn), lambda l, j: (l, 0, j)),
                  pl.BlockSpec((None, 1, tn), lambda l, j: (l, 0, j))],
        out_specs=pl.BlockSpec((None, MOD_ROWS, tn), lambda l, j: (l, 0, j)),
        out_shape=jax.ShapeDtypeStruct((depth, MOD_ROWS, n), F32),
        compiler_params=_params(2),
        name="modulation",
    )(cc, w_mod, b_mod.reshape(depth, 1, n))


def _rms(x):
    return x * lax.rsqrt(jnp.mean(x * x, axis=-1, keepdims=True) + NORM_EPS)


def _in_kernel(x_ref, mod_ref, g1_ref, w_ref, qg_ref, kg_ref, cos_ref, s1_ref, s2_ref,
               *refs, use_rope, kv_only):
    if kv_only:
        ka_ref, va_ref, kb_ref, vb_ref, h_ref = refs
    else:
        qa_ref, ka_ref, va_ref, qb_ref, kb_ref, vb_ref, cb_ref, cu_ref, gt_ref, h_ref = refs
    tm = x_ref.shape[0]
    nch = IN_CHAINS if tm % (256 * IN_CHAINS) == 0 else 1
    rc = tm // nch
    chains = [slice(c * rc, (c + 1) * rc) for c in range(nch)]
    for rs in chains:
        y = _rms(x_ref[rs, :]) * g1_ref[...]
        h_ref[rs, :] = (y * (1.0 + mod_ref[1:2, :]) + mod_ref[0:1, :]).astype(BF16)

    def proj(rs, c0, c1):
        return jnp.dot(h_ref[rs, :], w_ref[:, c0:c1], preferred_element_type=F32)

    low = lax.broadcasted_iota(jnp.int32, (rc, LANES), 1) < HEAD_DIM

    def rope(rs, t):
        if not use_rope:
            return t
        return (t * cos_ref[rs, :] + pltpu.roll(t, LANES - 16, 1) * s1_ref[rs, :]
                + pltpu.roll(t, 16, 1) * s2_ref[rs, :])

    def head_rms(t, g):
        tt = t * t
        s_lo = jnp.sum(jnp.where(low, tt, 0.0), axis=-1, keepdims=True)
        s_hi = jnp.sum(jnp.where(low, 0.0, tt), axis=-1, keepdims=True)
        ms = jnp.where(low, s_lo, s_hi) * (1.0 / HEAD_DIM)
        return t * lax.rsqrt(ms + NORM_EPS) * g

    def dup_halves(t):
        r = pltpu.roll(t, HEAD_DIM, 1)
        return jnp.where(low, t, r), jnp.where(low, r, t)

    group = lambda hd: slice(hd * LANES, (hd + 1) * LANES)
    if not kv_only:
        for rs in chains:
            z = proj(rs, C_AQ, C_AK)
            for hd in range(DA_HEADS):
                qa_ref[hd, rs, :] = (rope(rs, z[:, group(hd)]) * QK_SCALE).astype(BF16)
    for rs in chains:
        z = proj(rs, C_AK, C_AV)
        for hd in range(DA_HEADS):
            ka_ref[hd, rs, :] = rope(rs, z[:, group(hd)]).astype(BF16)
    if not kv_only:
        for rs in chains:
            z = proj(rs, C_BQ, C_BK)
            for p in range(4):
                t = head_rms(z[:, group(p)], qg_ref[...])
                qb_ref[p, rs, :] = (rope(rs, t) * QK_SCALE).astype(BF16)
    for rs in chains:
        z = proj(rs, C_BK, C_CB)
        k0, k1 = dup_halves(rope(rs, head_rms(z[:, :LANES], kg_ref[...])))
        kb_ref[0, rs, :] = k0.astype(BF16)
        kb_ref[1, rs, :] = k1.astype(BF16)
        v01 = z[:, LANES:]
        vb_ref[0, rs, :] = jnp.where(low, v01, 1.0).astype(BF16)
        vb_ref[1, rs, :] = jnp.where(low, pltpu.roll(v01, HEAD_DIM, 1), 1.0).astype(BF16)
    if not kv_only:
        for rs in chains:
            z = proj(rs, C_CB, C_G)
            cb_ref[rs, :] = z[:, :512]
            cu_ref[rs, :] = z[:, 512:1024] * z[:, 1024:1536]
        for j in range(3):
            c0 = C_G + j * D_MODEL
            for rs in chains:
                gt_ref[rs, j * D_MODEL:(j + 1) * D_MODEL] = jax.nn.sigmoid(
                    proj(rs, c0, c0 + D_MODEL))
    for rs in chains:
        z = proj(rs, C_AV, C_BQ)
        for hd in range(DA_HEADS):
            va_ref[hd, rs, :] = z[:, group(hd)].astype(BF16)


def _in_stage(x, mods, mod_row, layer, g1, w_in, qg, kg, tables, *, use_rope, kv_only=False):
    b, s, d = x.shape
    tm = min(TM_IN, s)
    cos_t, s1_t, s2_t = tables
    if mod_row is None:
        mod_map = lambda bi, i: (bi, 0, 0)
    else:
        mod_map = lambda bi, i: (mod_row, 0, 0)
    tok = lambda w: pl.BlockSpec((None, tm, w), lambda bi, i: (bi, i, 0))
    heads = lambda n: pl.BlockSpec((None, n, tm, LANES), lambda bi, i: (bi, 0, i, 0))
    tab = pl.BlockSpec((tm, LANES), lambda bi, i: (i, 0))
    hshape = lambda n: jax.ShapeDtypeStruct((b, n, s, LANES), BF16)
    if kv_only:
        out_specs = [heads(4), heads(4), heads(2), heads(2)]
        out_shape = [hshape(4), hshape(4), hshape(2), hshape(2)]
    else:
        out_specs = [heads(4), heads(4), heads(4), heads(4), heads(2), heads(2),
                     tok(512), tok(512), tok(3 * d)]
        out_shape = [hshape(4), hshape(4), hshape(4), hshape(4), hshape(2), hshape(2),
                     jax.ShapeDtypeStruct((b, s, 512), F32),
                     jax.ShapeDtypeStruct((b, s, 512), F32),
                     jax.ShapeDtypeStruct((b, s, 3 * d), F32)]
    return pl.pallas_call(
        functools.partial(_in_kernel, use_rope=use_rope, kv_only=kv_only),
        grid=(b, s // tm),
        in_specs=[tok(d),
                  pl.BlockSpec((None, N_MOD, d), mod_map),
                  _resident((1, d)), _resident_layer(w_in, layer),
                  _resident((1, LANES)), _resident((1, LANES)),
                  tab, tab, tab],
        out_specs=out_specs,
        out_shape=out_shape,
        scratch_shapes=[pltpu.VMEM((tm, d), BF16)],
        compiler_params=_params(2),
        name=("in_proj_rope" if use_rope else "in_proj_ctx") + ("_kv" if kv_only else ""),
    )(x, mods, g1, w_in, qg, kg, cos_t, s1_t, s2_t)


def _attn_kernel(*refs, mode, has_x, lam_init):
    if mode == "da":
        lam_ref, g_ref = refs[:2]
        refs = refs[2:]
    if has_x:
        q_ref, kc_ref, vc_ref, kx_ref, vx_ref, o_ref = refs[:6]
    else:
        q_ref, kc_ref, vc_ref, o_ref = refs[:4]
    s_ref = refs[-1]
    if mode == "da":
        acc_ref = refs[-2]
    hb, tq, _ = q_ref.shape
    rb = s_ref.shape[1] // 2
    nc = kc_ref.shape[1]
    low = lax.broadcasted_iota(jnp.int32, (rb, LANES), 1) < HEAD_DIM
    contract_last = (((1,), (1,)), ((), ()))
    if mode == "da":
        lv = lam_ref[...]
        lam = (jnp.exp(jnp.sum(lv[0:1] * lv[1:2], axis=-1, keepdims=True))
               - jnp.exp(jnp.sum(lv[2:3] * lv[3:4], axis=-1, keepdims=True)) + lam_init)

    def scores(blk):
        hd, r, kv, slot = blk
        q = q_ref[hd, r * rb:(r + 1) * rb, :]
        zero = jnp.zeros_like(q)
        qm = jnp.concatenate([jnp.where(low, q, zero), jnp.where(low, zero, q)], axis=0)
        s_ref[slot, :, 0:nc] = lax.dot_general(qm, kc_ref[kv], contract_last,
                                               preferred_element_type=F32)
        if has_x:
            s_ref[slot, :, nc:] = lax.dot_general(qm, kx_ref[kv], contract_last,
                                                  preferred_element_type=F32)

    def finish(blk):
        hd, r, kv, slot = blk
        def values(v):
            return jnp.concatenate([v, jnp.ones_like(v)], axis=1) if mode == "da" else v

        m = jnp.max(s_ref[slot], axis=-1, keepdims=True)
        pc = jnp.exp2(s_ref[slot, :, 0:nc] - m)
        o = jnp.dot(pc.astype(BF16), values(vc_ref[kv]), preferred_element_type=F32)
        if has_x:
            px = jnp.exp2(s_ref[slot, :, nc:] - m)
            o = o + jnp.dot(px.astype(BF16), values(vx_ref[kv]), preferred_element_type=F32)
        rows = slice(r * rb, (r + 1) * rb)
        if mode == "da":
            o = o[:, :LANES] / o[:, LANES:]
            acc_ref[0, rows, :] = o[:rb]
            acc_ref[1, rows, :] = o[rb:]
        else:
            o_lo, o_hi = o[:rb], o[rb:]
            res = jnp.where(low, o_lo / pltpu.roll(o_lo, HEAD_DIM, 1),
                            pltpu.roll(o_hi, HEAD_DIM, 1) / o_hi)
            o_ref[rows, hd * LANES:(hd + 1) * LANES] = res.astype(BF16)

    n_rows = tq // rb
    kv_of = (lambda hd: hd) if kc_ref.shape[0] == hb else (lambda hd: hd // 2)
    blocks = [(hd, r, kv_of(hd)) for hd in range(hb) for r in range(n_rows)]
    blocks = [(hd, r, kv, i % 2) for i, (hd, r, kv) in enumerate(blocks)]
    scores(blocks[0])
    for i, blk in enumerate(blocks):
        if i + 1 < len(blocks):
            scores(blocks[i + 1])
        finish(blk)
        hd, r = blk[0], blk[1]
        if mode == "da" and r == n_rows - 1:
            dlt = acc_ref[0] - lam * acc_ref[1]
            o_ref[:, hd * LANES:(hd + 1) * LANES] = (
                _rms(dlt) * g_ref[...] * (1.0 - lam_init)).astype(BF16)


def _attn_stage(q, kc, vc, kx, vx, *, mode, lam_vec=None, subln_g=None, lam_init=0.0):
    b, nh, s, _ = q.shape
    tq = min(TQ, s)
    rb = min(ATTN_ROWS[mode], tq)
    has_x = kx is not None
    hb = nh if s * nh <= TQ else 1
    kvb = hb if mode == "da" else max(1, hb // 2)
    kv_block = (lambda h: h) if (mode == "da" or hb > 1) else (lambda h: h // 2)
    q_spec = pl.BlockSpec((None, hb, tq, LANES), lambda bi, h, i: (bi, h, i, 0))
    kv_spec = lambda n: pl.BlockSpec((None, kvb, n, LANES), lambda bi, h, i: (bi, kv_block(h), 0, 0))
    args, specs = [], []
    if mode == "da":
        args += [lam_vec, subln_g]
        specs += [pl.BlockSpec(lam_vec.shape, lambda bi, h, i: (0, 0)),
                  pl.BlockSpec(subln_g.shape, lambda bi, h, i: (0, 0))]
    args += [q, kc, vc]
    specs += [q_spec, kv_spec(kc.shape[2]), kv_spec(kc.shape[2])]
    n_keys = kc.shape[2]
    if has_x:
        args += [kx, vx]
        specs += [kv_spec(kx.shape[2]), kv_spec(kx.shape[2])]
        n_keys += kx.shape[2]
    scratch = [pltpu.VMEM((2, 2 * rb, n_keys), F32)]
    if mode == "da":
        scratch = [pltpu.VMEM((2, tq, LANES), F32)] + scratch
    return pl.pallas_call(
        functools.partial(_attn_kernel, mode=mode, has_x=has_x, lam_init=lam_init),
        grid=(b, nh // hb, s // tq),
        in_specs=specs,
        out_specs=pl.BlockSpec((None, tq, hb * LANES), lambda bi, h, i: (bi, i, h)),
        out_shape=jax.ShapeDtypeStruct((b, s, nh * LANES), BF16),
        scratch_shapes=scratch,
        compiler_params=_params(3),
        name=f"attn_{mode}_{'x' if has_x else 'ctx'}",
    )(*args)


def _post_kernel(*refs, final, n_tiles):
    (x_ref, ya_ref, yb_ref, cb_ref, cu_ref, cup_ref, cun_ref, gt_ref, mod_ref, g2_ref, cw_ref,
     wa_ref, wb_ref, wc_ref, wo_ref, wgu_ref, wd_ref) = refs[:17]
    if final:
        gf_ref, o_ref = refs[17:]
    else:
        (o_ref,) = refs[17:]
    tm = x_ref.shape[0]
    i = pl.program_id(1)

    cu = cu_ref[...]
    row = lax.broadcasted_iota(jnp.int32, cu.shape, 0)
    prev_row = jnp.where(i > 0, cup_ref[7:8, :], 0.0)
    next_row = jnp.where(i < n_tiles - 1, cun_ref[0:1, :], 0.0)
    before = jnp.where(row == 0, prev_row, pltpu.roll(cu, 1, 0))
    after = jnp.where(row == tm - 1, next_row, pltpu.roll(cu, tm - 1, 0))
    yc = (cb_ref[...] * (before * cw_ref[0:1, :] + cu * cw_ref[1:2, :] + after * cw_ref[2:3, :])
          ).astype(BF16)

    d = D_MODEL
    nch = POST_CHAINS if tm % (128 * POST_CHAINS) == 0 else 1
    rc = tm // nch
    chains = [slice(c * rc, (c + 1) * rc) for c in range(nch)]
    dot = functools.partial(jnp.dot, preferred_element_type=F32)
    merged = [(gt_ref[rs, 0:d] * dot(ya_ref[rs, :], wa_ref[...])
               + gt_ref[rs, d:2 * d] * dot(yb_ref[rs, :], wb_ref[...])
               + gt_ref[rs, 2 * d:3 * d] * dot(yc[rs], wc_ref[...])).astype(BF16) for rs in chains]
    x1 = [x_ref[rs, :] + mod_ref[2:3, :] * dot(mg, wo_ref[...]) for rs, mg in zip(chains, merged)]
    h2 = [(_rms(v) * g2_ref[...] * (1.0 + mod_ref[4:5, :]) + mod_ref[3:4, :]).astype(BF16)
          for v in x1]
    act = []
    for h in h2:
        gu = dot(h, wgu_ref[...])
        gate = gu[:, :D_FF]
        act.append((gate * jax.nn.sigmoid(gate) * gu[:, D_FF:]).astype(BF16))
    for rs, v, a in zip(chains, x1, act):
        x2 = v + mod_ref[5:6, :] * dot(a, wd_ref[...])
        if final:
            x2 = _rms(x2) * gf_ref[...]
        o_ref[rs, :] = x2


def _post_stage(x, ya, yb, cb, cu, gates, mods, mod_row, layer, g2, conv_w, wa, wb, wc, wo, wgu,
                wd, final_g=None):
    b, s, d = x.shape
    tm = min(TM_POST, s)
    n_tiles = s // tm
    rows8 = tm // 8
    last8 = s // 8 - 1
    if mod_row is None:
        mod_map = lambda bi, i: (bi, 0, 0)
    else:
        mod_map = lambda bi, i: (mod_row, 0, 0)
    tok = lambda w: pl.BlockSpec((None, tm, w), lambda bi, i: (bi, i, 0))
    halo_prev = pl.BlockSpec((None, 8, 512), lambda bi, i: (bi, jnp.maximum(i * rows8 - 1, 0), 0))
    halo_next = pl.BlockSpec((None, 8, 512),
                             lambda bi, i: (bi, jnp.minimum((i + 1) * rows8, last8), 0))
    args = [x, ya, yb, cb, cu, cu, cu, gates, mods, g2, conv_w, wa, wb, wc, wo, wgu, wd]
    specs = [tok(d), tok(512), tok(512), tok(512), tok(512), halo_prev, halo_next, tok(3 * d),
             pl.BlockSpec((None, N_MOD, d), mod_map), _resident((1, d)), _resident(conv_w.shape)]
    specs += [_resident_layer(w, layer) for w in (wa, wb, wc, wo, wgu, wd)]
    final = final_g is not None
    if final:
        args.append(final_g)
        specs.append(_resident((1, d)))
    return pl.pallas_call(
        functools.partial(_post_kernel, final=final, n_tiles=n_tiles),
        grid=(b, n_tiles),
        in_specs=specs,
        out_specs=tok(d),
        out_shape=jax.ShapeDtypeStruct((b, s, d), F32),
        compiler_params=_params(2, POST_VMEM_LIMIT),
        name="post_final" if final else "post",
    )(*args)


def _rope_tables(seq):
    rows = seq // GRID_W
    row = jnp.repeat(jnp.arange(rows, dtype=F32), GRID_W)
    col = jnp.tile(jnp.arange(GRID_W, dtype=F32), rows)
    n_freq = HEAD_DIM // 4
    inv_freq = ROPE_THETA ** (-jnp.arange(n_freq, dtype=F32) / n_freq)
    ang_r, ang_c = row[:, None] * inv_freq, col[:, None] * inv_freq
    cr, sr, cc, sc = jnp.cos(ang_r), jnp.sin(ang_r), jnp.cos(ang_c), jnp.sin(ang_c)
    z = jnp.zeros_like(sr)
    rep = lambda parts: jnp.tile(jnp.concatenate(parts, axis=1), (1, LANES // HEAD_DIM))
    return rep([cr, cr, cc, cc]), rep([-sr, z, -sc, z]), rep([z, sr, z, sc])


def kernel(x, c, ctx, c_ctx, w_mod, b_mod, norm1_g, norm2_g, w_in, lam_q1, lam_k1, lam_q2, lam_k2,
           diff_subln_g, q_norm_g, k_norm_g, conv_w, w_branch_a, w_branch_b, w_branch_c, w_out,
           w_ffn_gu, w_ffn_down, final_g):
    batch, seq, d = x.shape
    ctx_row = batch
    cc = jnp.concatenate([c, c_ctx[None, :], jnp.zeros((MOD_ROWS - batch - 1, d), F32)], axis=0)
    mods_all = _modulation(cc, w_mod, b_mod).reshape(DEPTH, MOD_ROWS, N_MOD, d)
    tables = _rope_tables(seq)
    row1 = lambda v: v.reshape(1, -1)
    pair = lambda v: jnp.tile(v, LANES // HEAD_DIM).reshape(1, LANES)
    w_in_bf = w_in.astype(BF16)
    post_stacks = tuple(w.astype(BF16) for w in (w_branch_a, w_branch_b, w_branch_c, w_out,
                                                 w_ffn_gu, w_ffn_down))

    for l in range(DEPTH):
        lam_init = 0.8 - 0.6 * math.exp(-0.3 * l)
        mods = mods_all[l]
        lam_vec = jnp.stack([lam_q1[l], lam_k1[l], lam_q2[l], lam_k2[l]]).astype(F32)
        in_args = (l, row1(norm1_g[l]), w_in_bf, pair(q_norm_g[l]), pair(k_norm_g[l]), tables)
        post_w = (l, row1(norm2_g[l]), conv_w[l]) + post_stacks
        da = dict(mode="da", lam_vec=lam_vec, subln_g=row1(diff_subln_g[l]), lam_init=lam_init)

        last = l == DEPTH - 1
        if last:
            ka_c, va_c, kb_c, vb_c = _in_stage(ctx, mods, ctx_row, *in_args, use_rope=False,
                                               kv_only=True)
        else:
            qa_c, ka_c, va_c, qb_c, kb_c, vb_c, cb_c, cu_c, gt_c = _in_stage(
                ctx, mods, ctx_row, *in_args, use_rope=False)
        qa, ka, va, qb, kb, vb, cb, cu, gt = _in_stage(x, mods, None, *in_args, use_rope=True)

        ya = _attn_stage(qa, ka_c, va_c, ka, va, **da)
        yb = _attn_stage(qb, kb_c, vb_c, kb, vb, mode="gqa")
        x = _post_stage(x, ya, yb, cb, cu, gt, mods, None, *post_w,
                        final_g=row1(final_g) if last else None)
        if not last:
            ya_c = _attn_stage(qa_c, ka_c, va_c, None, None, **da)
            yb_c = _attn_stage(qb_c, kb_c, vb_c, None, None, mode="gqa")
            ctx = _post_stage(ctx, ya_c, yb_c, cb_c, cu_c, gt_c, mods, ctx_row, *post_w)
    return x
```

```python
import functools
import math

import jax
import jax.numpy as jnp
from jax import lax
from jax.experimental import pallas as pl
from jax.experimental.pallas import tpu as pltpu

F32 = jnp.float32
BF16 = jnp.bfloat16

D_MODEL = 1024
DEPTH = 2
GRID_W = 64
HEAD_DIM = 64
ROPE_THETA = 10000.0
NORM_EPS = 1e-6
DA_HEADS = 4
N_MOD = 6
D_FF = 2816
CONV_DIM = 512
LANES = 128
SUBLANES = 8
QK_SCALE = HEAD_DIM ** -0.5 * math.log2(math.e)

C_AQ, C_AK, C_AV, C_BQ, C_BK, C_CB, C_CC, C_CU, C_G = (
    0, 512, 1024, 1536, 2048, 2304, 2816, 3328, 3840)

MOD_TN = 1024
V7X_VMEM_BYTES = 64 * 1024 * 1024
VMEM_LIMIT = V7X_VMEM_BYTES - 8 * 1024 * 1024
POST_VMEM_LIMIT = V7X_VMEM_BYTES - 4 * 1024 * 1024

TM_IN = 512
IN_CHAINS = 2
TQ = 2048
ATTN_ROWS = {"da": 64, "gqa": 128}
TM_POST = 512
POST_CHAINS = 2


def _params(n_axes, vmem_limit=VMEM_LIMIT):
    return pltpu.CompilerParams(dimension_semantics=("arbitrary",) * n_axes,
                                vmem_limit_bytes=vmem_limit)


def _resident(shape):
    nd = len(shape)
    return pl.BlockSpec(shape, lambda *_: (0,) * nd, pipeline_mode=pl.Buffered(1))


def _resident_layer(stacked, layer):
    nd = stacked.ndim - 1
    return pl.BlockSpec((None,) + stacked.shape[1:], lambda *_: (layer,) + (0,) * nd,
                        pipeline_mode=pl.Buffered(1))


def _mod_kernel(cc_ref, w_ref, b_ref, o_ref):
    a = cc_ref[...]
    a = (a * jax.nn.sigmoid(a)).astype(BF16)
    o_ref[...] = jnp.dot(a, w_ref[...].astype(BF16), preferred_element_type=F32) + b_ref[...]


def _modulation(cc, w_mod, b_mod):
    depth, d, n = w_mod.shape
    rows, tn = cc.shape[0], MOD_TN
    return pl.pallas_call(
        _mod_kernel,
        grid=(depth, n // tn),
        in_specs=[pl.BlockSpec((rows, d), lambda l, j: (0, 0)),
                  pl.BlockSpec((None, d, tn), lambda l, j: (l, 0, j)),
                  pl.BlockSpec((None, 1, tn), lambda l, j: (l, 0, j))],
        out_specs=pl.BlockSpec((None, rows, tn), lambda l, j: (l, 0, j)),
        out_shape=jax.ShapeDtypeStruct((depth, rows, n), F32),
        compiler_params=_params(2),
        name="modulation",
    )(cc, w_mod, b_mod.reshape(depth, 1, n))


def _rms(x):
    return x * lax.rsqrt(jnp.mean(x * x, axis=-1, keepdims=True) + NORM_EPS)


def _in_kernel(x_ref, mod_ref, g1_ref, w_ref, qg_ref, kg_ref, cos_ref, s1_ref, s2_ref,
               *refs, use_rope, kv_only):
    if kv_only:
        ka_ref, va_ref, kb_ref, vb_ref, h_ref = refs
    else:
        qa_ref, ka_ref, va_ref, qb_ref, kb_ref, vb_ref, cb_ref, cu_ref, gt_ref, h_ref = refs
    tm = x_ref.shape[0]
    nch = IN_CHAINS if tm % (256 * IN_CHAINS) == 0 else 1
    rc = tm // nch
    chains = [slice(c * rc, (c + 1) * rc) for c in range(nch)]
    for rs in chains:
        y = _rms(x_ref[rs, :]) * g1_ref[...]
        h_ref[rs, :] = (y * (1.0 + mod_ref[1:2, :]) + mod_ref[0:1, :]).astype(BF16)

    def proj(rs, c0, c1):
        return jnp.dot(h_ref[rs, :], w_ref[:, c0:c1], preferred_element_type=F32)

    low = lax.broadcasted_iota(jnp.int32, (rc, LANES), 1) < HEAD_DIM

    def rope(rs, t):
        if not use_rope:
            return t
        return (t * cos_ref[rs, :] + pltpu.roll(t, LANES - 16, 1) * s1_ref[rs, :]
                + pltpu.roll(t, 16, 1) * s2_ref[rs, :])

    def head_rms(t, g):
        tt = t * t
        s_lo = jnp.sum(jnp.where(low, tt, 0.0), axis=-1, keepdims=True)
        s_hi = jnp.sum(jnp.where(low, 0.0, tt), axis=-1, keepdims=True)
        ms = jnp.where(low, s_lo, s_hi) * (1.0 / HEAD_DIM)
        return t * lax.rsqrt(ms + NORM_EPS) * g

    def dup_halves(t):
        r = pltpu.roll(t, HEAD_DIM, 1)
        return jnp.where(low, t, r), jnp.where(low, r, t)

    group = lambda hd: slice(hd * LANES, (hd + 1) * LANES)
    if not kv_only:
        for rs in chains:
            z = proj(rs, C_AQ, C_AK)
            for hd in range(DA_HEADS):
                qa_ref[hd, rs, :] = (rope(rs, z[:, group(hd)]) * QK_SCALE).astype(BF16)
    for rs in chains:
        z = proj(rs, C_AK, C_AV)
        for hd in range(DA_HEADS):
            ka_ref[hd, rs, :] = rope(rs, z[:, group(hd)]).astype(BF16)
    if not kv_only:
        for rs in chains:
            z = proj(rs, C_BQ, C_BK)
            for p in range(4):
                t = head_rms(z[:, group(p)], qg_ref[...])
                qb_ref[p, rs, :] = (rope(rs, t) * QK_SCALE).astype(BF16)
    for rs in chains:
        z = proj(rs, C_BK, C_CB)
        k0, k1 = dup_halves(rope(rs, head_rms(z[:, :LANES], kg_ref[...])))
        kb_ref[0, rs, :] = k0.astype(BF16)
        kb_ref[1, rs, :] = k1.astype(BF16)
        v01 = z[:, LANES:]
        vb_ref[0, rs, :] = jnp.where(low, v01, 1.0).astype(BF16)
        vb_ref[1, rs, :] = jnp.where(low, pltpu.roll(v01, HEAD_DIM, 1), 1.0).astype(BF16)
    if not kv_only:
        for rs in chains:
            z = proj(rs, C_CB, C_G)
            cb_ref[rs, :] = z[:, :CONV_DIM]
            cu_ref[rs, :] = z[:, CONV_DIM:2 * CONV_DIM] * z[:, 2 * CONV_DIM:]
        for j in range(3):
            c0 = C_G + j * D_MODEL
            for rs in chains:
                gt_ref[rs, j * D_MODEL:(j + 1) * D_MODEL] = jax.nn.sigmoid(
                    proj(rs, c0, c0 + D_MODEL))
    for rs in chains:
        z = proj(rs, C_AV, C_BQ)
        for hd in range(DA_HEADS):
            va_ref[hd, rs, :] = z[:, group(hd)].astype(BF16)


def _in_stage(x, mods, mod_row, layer, g1, w_in, qg, kg, tables, *, use_rope, kv_only=False):
    b, s, d = x.shape
    tm = min(TM_IN, s)
    cos_t, s1_t, s2_t = tables
    if mod_row is None:
        mod_map = lambda bi, i: (bi, 0, 0)
    else:
        mod_map = lambda bi, i: (mod_row, 0, 0)
    tok = lambda w: pl.BlockSpec((None, tm, w), lambda bi, i: (bi, i, 0))
    heads = lambda n: pl.BlockSpec((None, n, tm, LANES), lambda bi, i: (bi, 0, i, 0))
    tab = pl.BlockSpec((tm, LANES), lambda bi, i: (i, 0))
    hshape = lambda n: jax.ShapeDtypeStruct((b, n, s, LANES), BF16)
    if kv_only:
        out_specs = [heads(4), heads(4), heads(2), heads(2)]
        out_shape = [hshape(4), hshape(4), hshape(2), hshape(2)]
    else:
        out_specs = [heads(4), heads(4), heads(4), heads(4), heads(2), heads(2),
                     tok(CONV_DIM), tok(CONV_DIM), tok(3 * d)]
        out_shape = [hshape(4), hshape(4), hshape(4), hshape(4), hshape(2), hshape(2),
                     jax.ShapeDtypeStruct((b, s, CONV_DIM), F32),
                     jax.ShapeDtypeStruct((b, s, CONV_DIM), F32),
                     jax.ShapeDtypeStruct((b, s, 3 * d), F32)]
    return pl.pallas_call(
        functools.partial(_in_kernel, use_rope=use_rope, kv_only=kv_only),
        grid=(b, s // tm),
        in_specs=[tok(d),
                  pl.BlockSpec((None, N_MOD, d), mod_map),
                  _resident((1, d)), _resident_layer(w_in, layer),
                  _resident((1, LANES)), _resident((1, LANES)),
                  tab, tab, tab],
        out_specs=out_specs,
        out_shape=out_shape,
        scratch_shapes=[pltpu.VMEM((tm, d), BF16)],
        compiler_params=_params(2),
        name=("in_proj_rope" if use_rope else "in_proj_ctx") + ("_kv" if kv_only else ""),
    )(x, mods, g1, w_in, qg, kg, cos_t, s1_t, s2_t)


def _attn_kernel(*refs, mode, has_x, lam_init):
    if mode == "da":
        lam_ref, g_ref = refs[:2]
        refs = refs[2:]
    if has_x:
        q_ref, kc_ref, vc_ref, kx_ref, vx_ref, o_ref = refs[:6]
    else:
        q_ref, kc_ref, vc_ref, o_ref = refs[:4]
    s_ref = refs[-1]
    if mode == "da":
        acc_ref = refs[-2]
    hb, tq, _ = q_ref.shape
    rb = s_ref.shape[1] // 2
    nc = kc_ref.shape[1]
    low = lax.broadcasted_iota(jnp.int32, (rb, LANES), 1) < HEAD_DIM
    contract_last = (((1,), (1,)), ((), ()))
    if mode == "da":
        lv = lam_ref[...]
        lam = (jnp.exp(jnp.sum(lv[0:1] * lv[1:2], axis=-1, keepdims=True))
               - jnp.exp(jnp.sum(lv[2:3] * lv[3:4], axis=-1, keepdims=True)) + lam_init)

    def scores(blk):
        hd, r, kv, slot = blk
        q = q_ref[hd, r * rb:(r + 1) * rb, :]
        zero = jnp.zeros_like(q)
        qm = jnp.concatenate([jnp.where(low, q, zero), jnp.where(low, zero, q)], axis=0)
        s_ref[slot, :, 0:nc] = lax.dot_general(qm, kc_ref[kv], contract_last,
                                               preferred_element_type=F32)
        if has_x:
            s_ref[slot, :, nc:] = lax.dot_general(qm, kx_ref[kv], contract_last,
                                                  preferred_element_type=F32)

    def finish(blk):
        hd, r, kv, slot = blk
        def values(v):
            return jnp.concatenate([v, jnp.ones_like(v)], axis=1) if mode == "da" else v

        m = jnp.max(s_ref[slot], axis=-1, keepdims=True)
        pc = jnp.exp2(s_ref[slot, :, 0:nc] - m)
        o = jnp.dot(pc.astype(BF16), values(vc_ref[kv]), preferred_element_type=F32)
        if has_x:
            px = jnp.exp2(s_ref[slot, :, nc:] - m)
            o = o + jnp.dot(px.astype(BF16), values(vx_ref[kv]), preferred_element_type=F32)
        rows = slice(r * rb, (r + 1) * rb)
        if mode == "da":
            o = o[:, :LANES] / o[:, LANES:]
            acc_ref[0, rows, :] = o[:rb]
            acc_ref[1, rows, :] = o[rb:]
        else:
            o_lo, o_hi = o[:rb], o[rb:]
            res = jnp.where(low, o_lo / pltpu.roll(o_lo, HEAD_DIM, 1),
                            pltpu.roll(o_hi, HEAD_DIM, 1) / o_hi)
            o_ref[rows, hd * LANES:(hd + 1) * LANES] = res.astype(BF16)

    n_rows = tq // rb
    kv_of = (lambda hd: hd) if kc_ref.shape[0] == hb else (lambda hd: hd // 2)
    blocks = [(hd, r, kv_of(hd)) for hd in range(hb) for r in range(n_rows)]
    blocks = [(hd, r, kv, i % 2) for i, (hd, r, kv) in enumerate(blocks)]
    scores(blocks[0])
    for i, blk in enumerate(blocks):
        if i + 1 < len(blocks):
            scores(blocks[i + 1])
        finish(blk)
        hd, r = blk[0], blk[1]
        if mode == "da" and r == n_rows - 1:
            dlt = acc_ref[0] - lam * acc_ref[1]
            o_ref[:, hd * LANES:(hd + 1) * LANES] = (
                _rms(dlt) * g_ref[...] * (1.0 - lam_init)).astype(BF16)


def _attn_stage(q, kc, vc, kx, vx, *, mode, lam_vec=None, subln_g=None, lam_init=0.0):
    b, nh, s, _ = q.shape
    tq = min(TQ, s)
    rb = min(ATTN_ROWS[mode], tq)
    has_x = kx is not None
    hb = nh if s * nh <= TQ else 1
    kvb = hb if mode == "da" else max(1, hb // 2)
    kv_block = (lambda h: h) if (mode == "da" or hb > 1) else (lambda h: h // 2)
    q_spec = pl.BlockSpec((None, hb, tq, LANES), lambda bi, h, i: (bi, h, i, 0))
    kv_spec = lambda n: pl.BlockSpec((None, kvb, n, LANES), lambda bi, h, i: (bi, kv_block(h), 0, 0))
    args, specs = [], []
    if mode == "da":
        args += [lam_vec, subln_g]
        specs += [pl.BlockSpec(lam_vec.shape, lambda bi, h, i: (0, 0)),
                  pl.BlockSpec(subln_g.shape, lambda bi, h, i: (0, 0))]
    args += [q, kc, vc]
    specs += [q_spec, kv_spec(kc.shape[2]), kv_spec(kc.shape[2])]
    n_keys = kc.shape[2]
    if has_x:
        args += [kx, vx]
        specs += [kv_spec(kx.shape[2]), kv_spec(kx.shape[2])]
        n_keys += kx.shape[2]
    scratch = [pltpu.VMEM((2, 2 * rb, n_keys), F32)]
    if mode == "da":
        scratch = [pltpu.VMEM((2, tq, LANES), F32)] + scratch
    return pl.pallas_call(
        functools.partial(_attn_kernel, mode=mode, has_x=has_x, lam_init=lam_init),
        grid=(b, nh // hb, s // tq),
        in_specs=specs,
        out_specs=pl.BlockSpec((None, tq, hb * LANES), lambda bi, h, i: (bi, i, h)),
        out_shape=jax.ShapeDtypeStruct((b, s, nh * LANES), BF16),
        scratch_shapes=scratch,
        compiler_params=_params(3),
        name=f"attn_{mode}_{'x' if has_x else 'ctx'}",
    )(*args)


def _post_kernel(*refs, final, n_tiles):
    (x_ref, ya_ref, yb_ref, cb_ref, cu_ref, cup_ref, cun_ref, gt_ref, mod_ref, g2_ref, cw_ref,
     wa_ref, wb_ref, wc_ref, wo_ref, wgu_ref, wd_ref) = refs[:17]
    if final:
        gf_ref, o_ref = refs[17:]
    else:
        (o_ref,) = refs[17:]
    tm = x_ref.shape[0]
    i = pl.program_id(1)

    cu = cu_ref[...]
    row = lax.broadcasted_iota(jnp.int32, cu.shape, 0)
    prev_row = jnp.where(i > 0, cup_ref[SUBLANES - 1:SUBLANES, :], 0.0)
    next_row = jnp.where(i < n_tiles - 1, cun_ref[0:1, :], 0.0)
    before = jnp.where(row == 0, prev_row, pltpu.roll(cu, 1, 0))
    after = jnp.where(row == tm - 1, next_row, pltpu.roll(cu, tm - 1, 0))
    yc = (cb_ref[...] * (before * cw_ref[0:1, :] + cu * cw_ref[1:2, :] + after * cw_ref[2:3, :])
          ).astype(BF16)

    d = D_MODEL
    nch = POST_CHAINS if tm % (128 * POST_CHAINS) == 0 else 1
    rc = tm // nch
    chains = [slice(c * rc, (c + 1) * rc) for c in range(nch)]
    dot = functools.partial(jnp.dot, preferred_element_type=F32)
    merged = [(gt_ref[rs, 0:d] * dot(ya_ref[rs, :], wa_ref[...])
               + gt_ref[rs, d:2 * d] * dot(yb_ref[rs, :], wb_ref[...])
               + gt_ref[rs, 2 * d:3 * d] * dot(yc[rs], wc_ref[...])).astype(BF16) for rs in chains]
    x1 = [x_ref[rs, :] + mod_ref[2:3, :] * dot(mg, wo_ref[...]) for rs, mg in zip(chains, merged)]
    h2 = [(_rms(v) * g2_ref[...] * (1.0 + mod_ref[4:5, :]) + mod_ref[3:4, :]).astype(BF16)
          for v in x1]
    act = []
    for h in h2:
        gu = dot(h, wgu_ref[...])
        gate = gu[:, :D_FF]
        act.append((gate * jax.nn.sigmoid(gate) * gu[:, D_FF:]).astype(BF16))
    for rs, v, a in zip(chains, x1, act):
        x2 = v + mod_ref[5:6, :] * dot(a, wd_ref[...])
        if final:
            x2 = _rms(x2) * gf_ref[...]
        o_ref[rs, :] = x2


def _post_stage(x, ya, yb, cb, cu, gates, mods, mod_row, layer, g2, conv_w, wa, wb, wc, wo, wgu,
                wd, final_g=None):
    b, s, d = x.shape
    tm = min(TM_POST, s)
    n_tiles = s // tm
    rows8 = tm // SUBLANES
    last8 = s // SUBLANES - 1
    if mod_row is None:
        mod_map = lambda bi, i: (bi, 0, 0)
    else:
        mod_map = lambda bi, i: (mod_row, 0, 0)
    tok = lambda w: pl.BlockSpec((None, tm, w), lambda bi, i: (bi, i, 0))
    halo_prev = pl.BlockSpec((None, SUBLANES, CONV_DIM),
                             lambda bi, i: (bi, jnp.maximum(i * rows8 - 1, 0), 0))
    halo_next = pl.BlockSpec((None, SUBLANES, CONV_DIM),
                             lambda bi, i: (bi, jnp.minimum((i + 1) * rows8, last8), 0))
    args = [x, ya, yb, cb, cu, cu, cu, gates, mods, g2, conv_w, wa, wb, wc, wo, wgu, wd]
    specs = [tok(d), tok(CONV_DIM), tok(CONV_DIM), tok(CONV_DIM), tok(CONV_DIM),
             halo_prev, halo_next, tok(3 * d),
             pl.BlockSpec((None, N_MOD, d), mod_map), _resident((1, d)), _resident(conv_w.shape)]
    specs += [_resident_layer(w, layer) for w in (wa, wb, wc, wo, wgu, wd)]
    final = final_g is not None
    if final:
        args.append(final_g)
        specs.append(_resident((1, d)))
    return pl.pallas_call(
        functools.partial(_post_kernel, final=final, n_tiles=n_tiles),
        grid=(b, n_tiles),
        in_specs=specs,
        out_specs=tok(d),
        out_shape=jax.ShapeDtypeStruct((b, s, d), F32),
        compiler_params=_params(2, POST_VMEM_LIMIT),
        name="post_final" if final else "post",
    )(*args)


def _rope_tables(seq):
    rows = seq // GRID_W
    row = jnp.repeat(jnp.arange(rows, dtype=F32), GRID_W)
    col = jnp.tile(jnp.arange(GRID_W, dtype=F32), rows)
    n_freq = HEAD_DIM // 4
    inv_freq = ROPE_THETA ** (-jnp.arange(n_freq, dtype=F32) / n_freq)
    ang_r, ang_c = row[:, None] * inv_freq, col[:, None] * inv_freq
    cr, sr, cc, sc = jnp.cos(ang_r), jnp.sin(ang_r), jnp.cos(ang_c), jnp.sin(ang_c)
    z = jnp.zeros_like(sr)
    rep = lambda parts: jnp.tile(jnp.concatenate(parts, axis=1), (1, LANES // HEAD_DIM))
    return rep([cr, cr, cc, cc]), rep([-sr, z, -sc, z]), rep([z, sr, z, sc])


def kernel(x, c, ctx, c_ctx, w_mod, b_mod, norm1_g, norm2_g, w_in, lam_q1, lam_k1, lam_q2, lam_k2,
           diff_subln_g, q_norm_g, k_norm_g, conv_w, w_branch_a, w_branch_b, w_branch_c, w_out,
           w_ffn_gu, w_ffn_down, final_g):
    batch, seq, d = x.shape
    ctx_row = batch
    mod_rows = -(-(batch + 1) // SUBLANES) * SUBLANES
    cc = jnp.concatenate([c, c_ctx[None, :], jnp.zeros((mod_rows - batch - 1, d), F32)], axis=0)
    mods_all = _modulation(cc, w_mod, b_mod).reshape(DEPTH, mod_rows, N_MOD, d)
    tables = _rope_tables(seq)
    row1 = lambda v: v.reshape(1, -1)
    pair = lambda v: jnp.tile(v, LANES // HEAD_DIM).reshape(1, LANES)
    w_in_bf = w_in.astype(BF16)
    post_stacks = tuple(w.astype(BF16) for w in (w_branch_a, w_branch_b, w_branch_c, w_out,
                                                 w_ffn_gu, w_ffn_down))

    for l in range(DEPTH):
        lam_init = 0.8 - 0.6 * math.exp(-0.3 * l)
        mods = mods_all[l]
        lam_vec = jnp.stack([lam_q1[l], lam_k1[l], lam_q2[l], lam_k2[l]]).astype(F32)
        in_args = (l, row1(norm1_g[l]), w_in_bf, pair(q_norm_g[l]), pair(k_norm_g[l]), tables)
        post_w = (l, row1(norm2_g[l]), conv_w[l]) + post_stacks
        da = dict(mode="da", lam_vec=lam_vec, subln_g=row1(diff_subln_g[l]), lam_init=lam_init)

        last = l == DEPTH - 1
        if last:
            ka_c, va_c, kb_c, vb_c = _in_stage(ctx, mods, ctx_row, *in_args, use_rope=False,
                                               kv_only=True)
        else:
            qa_c, ka_c, va_c, qb_c, kb_c, vb_c, cb_c, cu_c, gt_c = _in_stage(
                ctx, mods, ctx_row, *in_args, use_rope=False)
        qa, ka, va, qb, kb, vb, cb, cu, gt = _in_stage(x, mods, None, *in_args, use_rope=True)

        ya = _attn_stage(qa, ka_c, va_c, ka, va, **da)
        yb = _attn_stage(qb, kb_c, vb_c, kb, vb, mode="gqa")
        x = _post_stage(x, ya, yb, cb, cu, gt, mods, None, *post_w,
                        final_g=row1(final_g) if last else None)
        if not last:
            ya_c = _attn_stage(qa_c, ka_c, va_c, None, None, **da)
            yb_c = _attn_stage(qb_c, kb_c, vb_c, None, None, mode="gqa")
            ctx = _post_stage(ctx, ya_c, yb_c, cb_c, cu_c, gt_c, mods, ctx_row, *post_w)
    return x
```
